```python
import jax, jax.numpy as jnp
from jax import lax
import numpy as np

D_MODEL = 1024
BATCH = 4
SEQ = 4096
DEPTH = 1

CHUNK = 64
EPS = 1e-6
D_FF = 2816
S5_WIDTH = D_MODEL
S5_GROUP = 16
S5_GROUPS = S5_WIDTH // S5_GROUP
S5_STATE = 64
D_INNER = 2 * D_MODEL
SSD_HEADDIM = 64
SSD_HEADS = D_INNER // SSD_HEADDIM
SSD_GROUPS = 8
SSD_HPG = SSD_HEADS // SSD_GROUPS
SSD_STATE = 128
CONV_K = 4
CONV_DIM = D_INNER + 2 * SSD_GROUPS * SSD_STATE
N_BRANCH = 2
IN_SPLITS = (S5_WIDTH, D_INNER, CONV_DIM, SSD_HEADS, N_BRANCH * D_MODEL)
D_IN = S5_WIDTH + D_INNER + CONV_DIM + SSD_HEADS + N_BRANCH * D_MODEL

kernel_name = "hybrid_s5_ssd_gated_macaron"


def rmsnorm(x, g):
    xf = x.astype(jnp.float32)
    y = xf * lax.rsqrt(jnp.mean(xf * xf, axis=-1, keepdims=True) + EPS)
    return (y * g.astype(jnp.float32)).astype(x.dtype)


def swiglu(h, w_gate, w_up, w_down):
    return (jax.nn.silu(h @ w_gate) * (h @ w_up)) @ w_down


def s5_branch(u, A_re, A_im, log_dt, B_re, B_im, C_re, C_im, d_skip, w_glu, b_glu):
    f32 = jnp.float32
    bsz, L, _ = u.shape
    uf = u.astype(f32).reshape(bsz, L, S5_GROUPS, S5_GROUP)
    dt = jnp.exp(log_dt.astype(f32))[:, None]
    lr, li = A_re.astype(f32), A_im.astype(f32)
    mag = jnp.exp(lr * dt)
    ar, ai = mag * jnp.cos(li * dt), mag * jnp.sin(li * dt)
    den = lr * lr + li * li
    cr = ((ar - 1.0) * lr + ai * li) / den
    ci = (ai * lr - (ar - 1.0) * li) / den
    Br, Bi = B_re.astype(f32), B_im.astype(f32)
    bbr = cr[..., None] * Br - ci[..., None] * Bi
    bbi = cr[..., None] * Bi + ci[..., None] * Br
    bu_r = jnp.einsum('blgm,gnm->blgn', uf, bbr)
    bu_i = jnp.einsum('blgm,gnm->blgn', uf, bbi)
    a_r = jnp.broadcast_to(ar, bu_r.shape)
    a_i = jnp.broadcast_to(ai, bu_i.shape)

    def combine(e1, e2):
        a1r, a1i, b1r, b1i = e1
        a2r, a2i, b2r, b2i = e2
        return (a2r * a1r - a2i * a1i,
                a2r * a1i + a2i * a1r,
                a2r * b1r - a2i * b1i + b2r,
                a2r * b1i + a2i * b1r + b2i)

    _, _, s_r, s_i = lax.associative_scan(combine, (a_r, a_i, bu_r, bu_i), axis=1)
    y = (jnp.einsum('blgn,gmn->blgm', s_r, C_re.astype(f32))
         - jnp.einsum('blgn,gmn->blgm', s_i, C_im.astype(f32)))
    y = y.reshape(bsz, L, S5_WIDTH) + d_skip.astype(f32) * u.astype(f32)
    g = jax.nn.gelu(y)
    out = g * jax.nn.sigmoid(g @ w_glu.astype(f32) + b_glu.astype(f32))
    return out.astype(u.dtype)


def causal_depthwise_conv(x, w, b):
    y = lax.conv_general_dilated(
        x, w[:, None, :].astype(x.dtype), window_strides=(1,),
        padding=[(CONV_K - 1, 0)], dimension_numbers=('NWC', 'WIO', 'NWC'),
        feature_group_count=CONV_DIM)
    return y + b.astype(x.dtype)


def ssd_branch(z, xbc, dt_raw, conv_w, conv_b, A_log, dt_bias, d_head, norm_w):
    f32 = jnp.float32
    bsz, L, _ = z.shape
    nc = L // CHUNK
    xbc = jax.nn.silu(causal_depthwise_conv(xbc, conv_w, conv_b))
    xs, Bm, Cm = jnp.split(xbc, [D_INNER, D_INNER + SSD_GROUPS * SSD_STATE], axis=-1)
    x = xs.reshape(bsz, nc, CHUNK, SSD_GROUPS, SSD_HPG, SSD_HEADDIM).astype(f32)
    Bm = Bm.reshape(bsz, nc, CHUNK, SSD_GROUPS, SSD_STATE).astype(f32)
    Cm = Cm.reshape(bsz, nc, CHUNK, SSD_GROUPS, SSD_STATE).astype(f32)
    dt = jax.nn.softplus(dt_raw.astype(f32) + dt_bias.astype(f32))
    dt = dt.reshape(bsz, nc, CHUNK, SSD_GROUPS, SSD_HPG)
    A = -jnp.exp(A_log.astype(f32)).reshape(SSD_GROUPS, SSD_HPG)
    a_cum = jnp.cumsum(dt * A, axis=2)
    seg = a_cum[:, :, :, None] - a_cum[:, :, None]
    mask = jnp.tril(jnp.ones((CHUNK, CHUNK), dtype=bool))[:, :, None, None]
    Lmat = jnp.exp(jnp.where(mask, seg, -jnp.inf))
    cb = jnp.einsum('bcign,bcjgn->bcijg', Cm, Bm)
    w = cb[..., None] * Lmat * dt[:, :, None]
    y_diag = jnp.einsum('bcijgk,bcjgkp->bcigkp', w, x)
    decay_to_end = jnp.exp(a_cum[:, :, -1:] - a_cum)
    xw = x * (decay_to_end * dt)[..., None]
    states = jnp.einsum('bclgn,bclgkp->bcgkpn', Bm, xw)
    chunk_decay = jnp.exp(a_cum[:, :, -1])

    def step(h, inp):
        dec, s = inp
        return dec[..., None, None] * h + s, h

    h0 = jnp.zeros((bsz, SSD_GROUPS, SSD_HPG, SSD_HEADDIM, SSD_STATE), f32)
    _, prev = lax.scan(step, h0, (jnp.moveaxis(chunk_decay, 1, 0), jnp.moveaxis(states, 1, 0)))
    prev = jnp.moveaxis(prev, 0, 1)
    y_off = jnp.einsum('bcign,bcgkpn->bcigkp', Cm, prev) * jnp.exp(a_cum)[..., None]
    y = y_diag + y_off + d_head.astype(f32).reshape(SSD_GROUPS, SSD_HPG)[:, :, None] * x
    y = y.reshape(bsz, L, D_INNER).astype(z.dtype)
    return rmsnorm(y * jax.nn.silu(z), norm_w)


def setup_inputs(seed: int = 0) -> dict:
    key = jax.random.key(seed)
    ks = iter(jax.random.split(key, 40))
    f32 = jnp.float32

    def nrm(shape, scale):
        return jax.random.normal(next(ks), shape, f32) * scale

    def gain(shape):
        return 1.0 + nrm(shape, 0.02)

    Dp = DEPTH
    x = jax.random.normal(next(ks), (BATCH, SEQ, D_MODEL), f32)
    inp = {"x": x}
    inp["ffn1_norm"] = gain((Dp, D_MODEL))
    inp["ffn1_w_gate"] = nrm((Dp, D_MODEL, D_FF), D_MODEL ** -0.5)
    inp["ffn1_w_up"] = nrm((Dp, D_MODEL, D_FF), D_MODEL ** -0.5)
    inp["ffn1_w_down"] = nrm((Dp, D_FF, D_MODEL), D_FF ** -0.5)
    inp["mix_norm"] = gain((Dp, D_MODEL))
    inp["w_in"] = nrm((Dp, D_MODEL, D_IN), D_MODEL ** -0.5)
    inp["conv_w"] = nrm((Dp, CONV_K, CONV_DIM), CONV_K ** -0.5)
    inp["conv_b"] = nrm((Dp, CONV_DIM), 0.02)
    inp["s5_A_re"] = -0.5 + nrm((Dp, S5_GROUPS, S5_STATE), 0.01)
    inp["s5_A_im"] = jnp.pi * jnp.arange(S5_STATE, dtype=f32) + nrm((Dp, S5_GROUPS, S5_STATE), 0.01)
    inp["s5_log_dt"] = jax.random.uniform(next(ks), (Dp, S5_GROUPS), f32,
                                          np.log(0.001), np.log(0.1))
    inp["s5_B_re"] = nrm((Dp, S5_GROUPS, S5_STATE, S5_GROUP), (2 * S5_GROUP) ** -0.5)
    inp["s5_B_im"] = nrm((Dp, S5_GROUPS, S5_STATE, S5_GROUP), (2 * S5_GROUP) ** -0.5)
    inp["s5_C_re"] = nrm((Dp, S5_GROUPS, S5_GROUP, S5_STATE), S5_STATE ** -0.5)
    inp["s5_C_im"] = nrm((Dp, S5_GROUPS, S5_GROUP, S5_STATE), S5_STATE ** -0.5)
    inp["s5_D"] = nrm((Dp, S5_WIDTH), 1.0)
    inp["s5_w_glu"] = nrm((Dp, S5_WIDTH, S5_WIDTH), S5_WIDTH ** -0.5)
    inp["s5_b_glu"] = nrm((Dp, S5_WIDTH), 0.02)
    inp["ssd_A_log"] = jnp.log(jax.random.uniform(next(ks), (Dp, SSD_HEADS), f32, 1.0, 16.0))
    dt0 = jnp.exp(jax.random.uniform(next(ks), (Dp, SSD_HEADS), f32, np.log(0.001), np.log(0.1)))
    inp["ssd_dt_bias"] = dt0 + jnp.log(-jnp.expm1(-dt0))
    inp["ssd_D"] = gain((Dp, SSD_HEADS))
    inp["ssd_norm"] = gain((Dp, D_INNER))
    inp["w_proj_s5"] = nrm((Dp, S5_WIDTH, D_MODEL), S5_WIDTH ** -0.5)
    inp["w_proj_ssd"] = nrm((Dp, D_INNER, D_MODEL), D_INNER ** -0.5)
    inp["b_gate"] = nrm((Dp, N_BRANCH * D_MODEL), 0.02)
    inp["w_out"] = nrm((Dp, D_MODEL, D_MODEL), D_MODEL ** -0.5)
    inp["ffn2_norm"] = gain((Dp, D_MODEL))
    inp["ffn2_w_gate"] = nrm((Dp, D_MODEL, D_FF), D_MODEL ** -0.5)
    inp["ffn2_w_up"] = nrm((Dp, D_MODEL, D_FF), D_MODEL ** -0.5)
    inp["ffn2_w_down"] = nrm((Dp, D_FF, D_MODEL), D_FF ** -0.5)
    inp["final_norm"] = gain((D_MODEL,))
    return inp


def reference(x, ffn1_norm, ffn1_w_gate, ffn1_w_up, ffn1_w_down, mix_norm, w_in,
              conv_w, conv_b, s5_A_re, s5_A_im, s5_log_dt, s5_B_re, s5_B_im,
              s5_C_re, s5_C_im, s5_D, s5_w_glu, s5_b_glu, ssd_A_log, ssd_dt_bias,
              ssd_D, ssd_norm, w_proj_s5, w_proj_ssd, b_gate, w_out,
              ffn2_norm, ffn2_w_gate, ffn2_w_up, ffn2_w_down, final_norm):
    split_at = [sum(IN_SPLITS[:i + 1]) for i in range(len(IN_SPLITS) - 1)]
    for i in range(DEPTH):
        x = x + 0.5 * swiglu(rmsnorm(x, ffn1_norm[i]), ffn1_w_gate[i], ffn1_w_up[i], ffn1_w_down[i])
        h = rmsnorm(x, mix_norm[i])
        proj = h @ w_in[i]
        u_s5, z, xbc, dt_raw, gate_logits = jnp.split(proj, split_at, axis=-1)
        y_s5 = s5_branch(u_s5, s5_A_re[i], s5_A_im[i], s5_log_dt[i], s5_B_re[i], s5_B_im[i],
                         s5_C_re[i], s5_C_im[i], s5_D[i], s5_w_glu[i], s5_b_glu[i])
        y_ssd = ssd_branch(z, xbc, dt_raw, conv_w[i], conv_b[i], ssd_A_log[i], ssd_dt_bias[i],
                           ssd_D[i], ssd_norm[i])
        gates = jax.nn.sigmoid(gate_logits + b_gate[i])
        g_s5, g_ssd = jnp.split(gates, 2, axis=-1)
        merged = g_s5 * (y_s5 @ w_proj_s5[i]) + g_ssd * (y_ssd @ w_proj_ssd[i])
        x = x + merged @ w_out[i]
        x = x + 0.5 * swiglu(rmsnorm(x, ffn2_norm[i]), ffn2_w_gate[i], ffn2_w_up[i], ffn2_w_down[i])
    return rmsnorm(x, final_norm)
```

```python
import functools

import jax
import jax.numpy as jnp
import numpy as np
from jax import lax
from jax.experimental import pallas as pl
from jax.experimental.pallas import tpu as pltpu

EPS = 1e-6
D_MODEL = 1024
D_FF = 2816
S5_GROUP = 16
S5_GROUPS = 64
S5_STATE = 64
D_INNER = 2048
HEADDIM = 64
HEADS = 32
SSD_GROUPS = 8
HPG = 4
SSD_STATE = 128
CONV_K = 4
CONV_DIM = 4096
BC_WIDTH = SSD_GROUPS * SSD_STATE
GROUP_COLS = HPG * HEADDIM

S5_CHUNK = 32
SSD_CHUNK = 64
DT_PAD = 128

VMEM_LIMIT = 56 * 1024 * 1024

F32 = jnp.float32
BF16 = jnp.bfloat16


def _dot(a, b):
    return jnp.dot(a, b, preferred_element_type=F32)


def _sigmoid(v):
    return 1.0 / (1.0 + jnp.exp(-v))


def _silu(v):
    return v * _sigmoid(v)


def _rms_scale(xf):
    return lax.rsqrt(jnp.mean(xf * xf, axis=-1, keepdims=True) + EPS)


def _split3(v):
    hi = v.astype(BF16)
    r1 = v - hi.astype(F32)
    mid = r1.astype(BF16)
    lo = (r1 - mid.astype(F32)).astype(BF16)
    return hi, mid, lo


def _dot_exact_rhs(v, m_bf16):
    hi, mid, lo = _split3(v)
    return _dot(hi, m_bf16) + _dot(mid, m_bf16) + _dot(lo, m_bf16)


def _whole(shape=None):
    return pl.BlockSpec(memory_space=pltpu.VMEM)


def _params(sem):
    return pltpu.CompilerParams(dimension_semantics=sem, vmem_limit_bytes=VMEM_LIMIT)


def _ffn_body(x_ref, g_ref, wg_ref, wu_ref, wd_ref, fin_ref, o_ref, *, n_chunks, final):
    x = x_ref[...]
    h = (x * _rms_scale(x) * g_ref[...]).astype(BF16)
    width = D_FF // n_chunks
    acc = jnp.zeros(x.shape, F32)
    for c in range(n_chunks):
        sl = slice(c * width, (c + 1) * width)
        a = _dot(h, wg_ref[:, sl])
        b = _dot(h, wu_ref[:, sl])
        act = (_silu(a) * b).astype(BF16)
        acc = acc + _dot(act, wd_ref[sl, :])
    y = x + 0.5 * acc
    if final:
        y = y * _rms_scale(y) * fin_ref[...]
    o_ref[...] = y


def _ffn(x, g, wg, wu, wd, fin, *, final, tm=512):
    t = x.shape[0]
    tm = min(tm, t)
    tile = pl.BlockSpec((tm, D_MODEL), lambda i: (i, 0))
    return pl.pallas_call(
        functools.partial(_ffn_body, n_chunks=2, final=final),
        grid=(t // tm,),
        in_specs=[tile, _whole(), _whole(), _whole(), _whole(), _whole()],
        out_specs=tile,
        out_shape=jax.ShapeDtypeStruct((t, D_MODEL), F32),
        compiler_params=_params(("parallel",)),
        name="ffn_final" if final else "ffn",
    )(x, g, wg, wu, wd, fin)


def _inproj_body(x_ref, g_ref, wu_ref, wz_ref, wx_ref, wdt_ref, wgt_ref, bg_ref,
                 u_ref, z_ref, xbc_ref, dt_ref, gate_ref):
    x = x_ref[...]
    h = (x * _rms_scale(x) * g_ref[...]).astype(BF16)
    u_ref[...] = _dot(h, wu_ref[...])
    z_ref[...] = _dot(h, wz_ref[...])
    xbc_ref[...] = _dot(h, wx_ref[...])
    dt_ref[...] = _dot(h, wdt_ref[...])
    gate_ref[...] = _sigmoid(_dot(h, wgt_ref[...]) + bg_ref[...])


def _inproj(x, g, wu, wz, wx, wdt, wgt, bg, *, tm=256):
    t = x.shape[0]
    tm = min(tm, t)

    def tile(c):
        return pl.BlockSpec((tm, c), lambda i: (i, 0))

    widths = (D_MODEL, D_INNER, CONV_DIM, DT_PAD, 2 * D_MODEL)
    return pl.pallas_call(
        _inproj_body,
        grid=(t // tm,),
        in_specs=[tile(D_MODEL)] + [_whole()] * 7,
        out_specs=[tile(c) for c in widths],
        out_shape=[jax.ShapeDtypeStruct((t, c), F32) for c in widths],
        compiler_params=_params(("parallel",)),
        name="inproj",
    )(x, g, wu, wz, wx, wdt, wgt, bg)


def _s5_taps_body(c_ref, w_ref, k_ref):
    k_ref[0] = jnp.dot(c_ref[0], w_ref[0], preferred_element_type=F32,
                       precision=lax.Precision.HIGHEST)


def _s5_taps(c_cat, w_cat):
    g, m, n2 = c_cat.shape
    cols = w_cat.shape[-1]
    return pl.pallas_call(
        _s5_taps_body,
        grid=(g,),
        in_specs=[pl.BlockSpec((1, m, n2), lambda i: (i, 0, 0)),
                  pl.BlockSpec((1, n2, cols), lambda i: (i, 0, 0))],
        out_specs=pl.BlockSpec((1, m, cols), lambda i: (i, 0, 0)),
        out_shape=jax.ShapeDtypeStruct((g, m, cols), F32),
        compiler_params=_params(("parallel",)),
        name="s5_taps",
    )(c_cat, w_cat)


def _s5_main_body(u_ref, tt_ref, p_ref, q_ref, apow_ref, y_ref, *, n_chunks, n_steps):
    u = u_ref[0]
    rows = u.shape[0]
    v = _dot(u, p_ref[0])
    cidx = lax.broadcasted_iota(jnp.int32, (rows, 2 * S5_STATE), 0) % n_chunks
    x = v
    for j in range(n_steps):
        d = 1 << j
        xs = jnp.where(cidx >= d, pltpu.roll(x, d, axis=0), 0.0)
        x = x + apow_ref[0, j, 0:1, :] * xs + apow_ref[0, j, 1:2, :] * pltpu.roll(xs, S5_STATE, axis=1)
    s_prev = jnp.where(cidx >= 1, pltpu.roll(x, 1, axis=0), 0.0)
    y_ref[0] = _dot(u, tt_ref[0]) + _dot(s_prev.astype(BF16), q_ref[0])


def _s5_main(u_g, tt, p, q, apow, *, n_chunks):
    g, rows, width = u_g.shape
    n_steps = apow.shape[1]
    return pl.pallas_call(
        functools.partial(_s5_main_body, n_chunks=n_chunks, n_steps=n_steps),
        grid=(g,),
        in_specs=[pl.BlockSpec((1, rows, width), lambda i: (i, 0, 0)),
                  pl.BlockSpec((1, width, width), lambda i: (i, 0, 0)),
                  pl.BlockSpec((1, width, 2 * S5_STATE), lambda i: (i, 0, 0)),
                  pl.BlockSpec((1, 2 * S5_STATE, width), lambda i: (i, 0, 0)),
                  pl.BlockSpec((1, n_steps, 2, 2 * S5_STATE), lambda i: (i, 0, 0, 0))],
        out_specs=pl.BlockSpec((1, rows, width), lambda i: (i, 0, 0)),
        out_shape=jax.ShapeDtypeStruct((g, rows, width), F32),
        compiler_params=_params(("parallel",)),
        name="s5_main",
    )(u_g, tt, p, q, apow)


def _s5_operators(a_re, a_im, log_dt, b_re, b_im, c_re, c_im, n_chunks):
    lc = S5_CHUNK
    dt = jnp.exp(log_dt)[:, None]
    lr, li = a_re, a_im
    mag = jnp.exp(lr * dt)
    ar, ai = mag * jnp.cos(li * dt), mag * jnp.sin(li * dt)
    den = lr * lr + li * li
    cr = ((ar - 1.0) * lr + ai * li) / den
    ci = (ai * lr - (ar - 1.0) * li) / den
    bbr = cr[..., None] * b_re - ci[..., None] * b_im
    bbi = cr[..., None] * b_im + ci[..., None] * b_re

    def power(k):
        kk = k[None, :, None]
        m = jnp.exp(kk * (lr * dt)[:, None, :])
        ph = kk * (li * dt)[:, None, :]
        return m * jnp.cos(ph), m * jnp.sin(ph)

    pr, pi = power(jnp.arange(lc + 1, dtype=F32))
    wr = pr[:, :lc, :, None] * bbr[:, None] - pi[:, :lc, :, None] * bbi[:, None]
    wi = pr[:, :lc, :, None] * bbi[:, None] + pi[:, :lc, :, None] * bbr[:, None]
    g = S5_GROUPS
    w_cat = jnp.concatenate([wr.transpose(0, 2, 1, 3).reshape(g, S5_STATE, lc * S5_GROUP),
                             wi.transpose(0, 2, 1, 3).reshape(g, S5_STATE, lc * S5_GROUP)], axis=1)
    c_cat = jnp.concatenate([c_re, -c_im], axis=-1)
    p = jnp.concatenate([wr[:, ::-1].transpose(0, 1, 3, 2).reshape(g, lc * S5_GROUP, S5_STATE),
                         wi[:, ::-1].transpose(0, 1, 3, 2).reshape(g, lc * S5_GROUP, S5_STATE)], axis=-1)
    p1r, p1i = pr[:, 1:], pi[:, 1:]
    qr = c_re[:, None] * p1r[:, :, None, :] - c_im[:, None] * p1i[:, :, None, :]
    qi = -(c_re[:, None] * p1i[:, :, None, :] + c_im[:, None] * p1r[:, :, None, :])
    q = jnp.concatenate([qr.transpose(0, 3, 1, 2).reshape(g, S5_STATE, lc * S5_GROUP),
                         qi.transpose(0, 3, 1, 2).reshape(g, S5_STATE, lc * S5_GROUP)], axis=1)
    n_steps = max(1, int(np.ceil(np.log2(n_chunks)))) if n_chunks > 1 else 1
    sr, si = power(jnp.asarray([lc * (1 << j) for j in range(n_steps)], F32))
    apow = jnp.stack([jnp.concatenate([sr, sr], -1), jnp.concatenate([-si, si], -1)], axis=2)
    return c_cat, w_cat, p.astype(BF16), q.astype(BF16), apow


def _s5_toeplitz(taps):
    lc = S5_CHUNK
    g = taps.shape[0]
    k4 = taps.reshape(g, S5_GROUP, lc, S5_GROUP).transpose(0, 2, 3, 1)
    s_idx = np.arange(lc)[:, None]
    t_idx = np.arange(lc)[None, :]
    lag = np.clip(t_idx - s_idx, 0, lc - 1)
    tt = k4[:, lag]
    tt = jnp.where((t_idx >= s_idx)[None, :, :, None, None], tt, 0.0)
    return tt.transpose(0, 1, 3, 2, 4).reshape(g, lc * S5_GROUP, lc * S5_GROUP).astype(BF16)


def _ssd_body(xbc_ref, z_ref, dt_ref, cw_ref, cb_ref, alog_ref, dtb_ref, dexp_ref, nw_ref,
              e_ref, tri_ref, y_ref, buf_ref, h_ref):
    lc = SSD_CHUNK
    c = pl.program_id(1)

    @pl.when(c == 0)
    def _():
        buf_ref[0:8, :] = jnp.zeros((8, CONV_DIM), F32)
        h_ref[...] = jnp.zeros(h_ref.shape, F32)

    @pl.when(c > 0)
    def _():
        buf_ref[0:8, :] = buf_ref[lc:lc + 8, :]

    buf_ref[8:8 + lc, :] = xbc_ref[...]
    acc = cb_ref[...] + cw_ref[CONV_K - 1:CONV_K, :] * buf_ref[8:8 + lc, :]
    for k in range(CONV_K - 1):
        off = 8 - (CONV_K - 1) + k
        acc = acc + cw_ref[k:k + 1, :] * buf_ref[off:off + lc, :]
    act = _silu(acc)
    xs = act[:, :D_INNER]
    bm = act[:, D_INNER:D_INNER + BC_WIDTH].astype(BF16)
    cm = act[:, D_INNER + BC_WIDTH:].astype(BF16)

    raw = dt_ref[:, :HEADS] + dtb_ref[...]
    dt = jnp.maximum(raw, 0.0) + jnp.log1p(jnp.exp(-jnp.abs(raw)))
    da = dt * (-jnp.exp(alog_ref[...]))
    hi, mid, lo = _split3(da)
    tri = tri_ref[...]
    a_cum = _dot(tri, hi) + _dot(tri, mid) + _dot(tri, lo)

    e = e_ref[...]
    a_e = _dot_exact_rhs(a_cum, e)
    dt_e = _dot_exact_rhs(dt, e)
    row = lax.broadcasted_iota(jnp.int32, (lc, D_INNER), 0)
    lane_j = lax.broadcasted_iota(jnp.int32, (lc, D_INNER), 1) % lc
    diag = row == lane_j
    a_j = jnp.sum(jnp.where(diag, a_e, 0.0), axis=0, keepdims=True)
    dt_j = jnp.sum(jnp.where(diag, dt_e, 0.0), axis=0, keepdims=True)
    lmat = jnp.exp(jnp.where(lane_j <= row, a_e - a_j, -jnp.inf))
    a_last = a_e[lc - 1:lc, :]
    exp_a = jnp.exp(a_e)
    xw = (xs * (jnp.exp(a_last - a_e) * dt_e)).astype(BF16)
    chunk_decay = jnp.exp(a_last)
    xs_b = xs.astype(BF16)

    blk_r = lax.broadcasted_iota(jnp.int32, (GROUP_COLS, GROUP_COLS), 0) // HEADDIM
    blk_c = lax.broadcasted_iota(jnp.int32, (GROUP_COLS, GROUP_COLS), 1) // HEADDIM
    same_head = blk_r == blk_c
    nt = (((1,), (1,)), ((), ()))
    tn = (((0,), (0,)), ((), ()))
    for g in range(SSD_GROUPS):
        cs = slice(g * GROUP_COLS, (g + 1) * GROUP_COLS)
        ns = slice(g * SSD_STATE, (g + 1) * SSD_STATE)
        bm_g, cm_g = bm[:, ns], cm[:, ns]
        cb = lax.dot_general(cm_g, jnp.concatenate([bm_g] * HPG, axis=0), nt,
                             preferred_element_type=F32)
        w = (cb * lmat[:, cs] * dt_j[:, cs]).astype(BF16)
        x_g = xs_b[:, cs]
        x_bd = jnp.where(same_head, jnp.concatenate([x_g] * HPG, axis=0), jnp.zeros((), BF16))
        y_diag = _dot(w, x_bd)
        h_prev = h_ref[g]
        y_off = _dot(cm_g, h_prev.astype(BF16)) * exp_a[:, cs]
        y_ref[:, cs] = y_diag + y_off
        st = lax.dot_general(bm_g, xw[:, cs], tn, preferred_element_type=F32)
        h_ref[g] = h_prev * chunk_decay[:, cs] + st

    y = y_ref[...] + dexp_ref[...] * xs
    t = y * _silu(z_ref[...])
    y_ref[...] = t * _rms_scale(t) * nw_ref[...]


def _ssd(xbc, z, dt, cw, cb, alog, dtb, dexp, nw, e, tri, *, batch, n_chunks):
    lc = SSD_CHUNK
    t = xbc.shape[0]

    def tile(c):
        return pl.BlockSpec((lc, c), lambda b, i: (b * n_chunks + i, 0))

    return pl.pallas_call(
        _ssd_body,
        grid=(batch, n_chunks),
        in_specs=[tile(CONV_DIM), tile(D_INNER), tile(DT_PAD)] + [_whole()] * 8,
        out_specs=tile(D_INNER),
        out_shape=jax.ShapeDtypeStruct((t, D_INNER), F32),
        scratch_shapes=[pltpu.VMEM((lc + 8, CONV_DIM), F32),
                        pltpu.VMEM((SSD_GROUPS, SSD_STATE, GROUP_COLS), F32)],
        compiler_params=_params(("parallel", "arbitrary")),
        name="ssd",
    )(xbc, z, dt, cw, cb, alog, dtb, dexp, nw, e, tri)


def _merge_body(x_ref, y5_ref, u_ref, yssd_ref, gate_ref, dskip_ref, wglu_ref, bglu_ref,
                wp5_ref, wpssd_ref, wout_ref, o_ref):
    y = y5_ref[...] + dskip_ref[...] * u_ref[...]
    gl = 0.5 * y * (1.0 + jnp.tanh(np.sqrt(2.0 / np.pi).astype(np.float32) * (y + 0.044715 * (y * y * y))))
    glu = gl * _sigmoid(_dot(gl.astype(BF16), wglu_ref[...]) + bglu_ref[...])
    p5 = _dot(glu.astype(BF16), wp5_ref[...])
    pssd = _dot(yssd_ref[...].astype(BF16), wpssd_ref[...])
    merged = gate_ref[:, :D_MODEL] * p5 + gate_ref[:, D_MODEL:] * pssd
    o_ref[...] = x_ref[...] + _dot(merged.astype(BF16), wout_ref[...])


def _merge(x, y5, u, yssd, gates, dskip, wglu, bglu, wp5, wpssd, wout, *, tm=256):
    t = x.shape[0]
    tm = min(tm, t)

    def tile(c):
        return pl.BlockSpec((tm, c), lambda i: (i, 0))

    return pl.pallas_call(
        _merge_body,
        grid=(t // tm,),
        in_specs=[tile(D_MODEL), tile(D_MODEL), tile(D_MODEL), tile(D_INNER), tile(2 * D_MODEL)]
        + [_whole()] * 6,
        out_specs=tile(D_MODEL),
        out_shape=jax.ShapeDtypeStruct((t, D_MODEL), F32),
        compiler_params=_params(("parallel",)),
        name="merge",
    )(x, y5, u, yssd, gates, dskip, wglu, bglu, wp5, wpssd, wout)


def kernel(x, ffn1_norm, ffn1_w_gate, ffn1_w_up, ffn1_w_down, mix_norm, w_in, conv_w, conv_b, s5_A_re, s5_A_im, s5_log_dt, s5_B_re, s5_B_im, s5_C_re, s5_C_im, s5_D, s5_w_glu, s5_b_glu, ssd_A_log, ssd_dt_bias, ssd_D, ssd_norm, w_proj_s5, w_proj_ssd, b_gate, w_out, ffn2_norm, ffn2_w_gate, ffn2_w_up, ffn2_w_down, final_norm):
    bsz, seq, _ = x.shape
    t = bsz * seq
    depth = ffn1_norm.shape[0]
    xt = x.reshape(t, D_MODEL)
    row = lambda v: v.reshape(1, -1).astype(F32)
    fin = row(final_norm)

    e_mat = jnp.asarray(np.repeat(np.eye(HEADS, dtype=np.float32), HEADDIM, axis=1), BF16)
    tri = jnp.asarray(np.tril(np.ones((SSD_CHUNK, SSD_CHUNK), np.float32)), BF16)

    o_u, o_z, o_x, o_dt = D_MODEL, D_MODEL + D_INNER, D_MODEL + D_INNER + CONV_DIM, D_MODEL + D_INNER + CONV_DIM + HEADS
    s5_nc = seq // S5_CHUNK
    ssd_nc = seq // SSD_CHUNK

    for i in range(depth):
        xt = _ffn(xt, row(ffn1_norm[i]), ffn1_w_gate[i].astype(BF16), ffn1_w_up[i].astype(BF16),
                  ffn1_w_down[i].astype(BF16), fin, final=False)

        w = w_in[i]
        w_dt = jnp.pad(w[:, o_x:o_dt], ((0, 0), (0, DT_PAD - HEADS)))
        u, z, xbc, dt_raw, gates = _inproj(
            xt, row(mix_norm[i]), w[:, :o_u].astype(BF16), w[:, o_u:o_z].astype(BF16),
            w[:, o_z:o_x].astype(BF16), w_dt.astype(BF16), w[:, o_dt:].astype(BF16), row(b_gate[i]))

        c_cat, w_cat, p_op, q_op, apow = _s5_operators(
            s5_A_re[i], s5_A_im[i], s5_log_dt[i], s5_B_re[i], s5_B_im[i], s5_C_re[i], s5_C_im[i], s5_nc)
        tt = _s5_toeplitz(_s5_taps(c_cat, w_cat))
        u_g = (u.astype(BF16).reshape(bsz * s5_nc, S5_CHUNK, S5_GROUPS, S5_GROUP)
               .transpose(2, 0, 1, 3).reshape(S5_GROUPS, bsz * s5_nc, S5_CHUNK * S5_GROUP))
        y_g = _s5_main(u_g, tt, p_op, q_op, apow, n_chunks=s5_nc)
        y5 = (y_g.reshape(S5_GROUPS, bsz * s5_nc, S5_CHUNK, S5_GROUP)
              .transpose(1, 2, 0, 3).reshape(t, D_MODEL))

        yssd = _ssd(xbc, z, dt_raw, conv_w[i], row(conv_b[i]), row(ssd_A_log[i]), row(ssd_dt_bias[i]),
                    row(jnp.repeat(ssd_D[i], HEADDIM)), row(ssd_norm[i]), e_mat, tri,
                    batch=bsz, n_chunks=ssd_nc)

        xt = _merge(xt, y5, u, yssd, gates, row(s5_D[i]), s5_w_glu[i].astype(BF16), row(s5_b_glu[i]),
                    w_proj_s5[i].astype(BF16), w_proj_ssd[i].astype(BF16), w_out[i].astype(BF16))

        last = i == depth - 1
        xt = _ffn(xt, row(ffn2_norm[i]), ffn2_w_gate[i].astype(BF16), ffn2_w_up[i].astype(BF16),
                  ffn2_w_down[i].astype(BF16), fin, final=last)
    return xt.reshape(bsz, seq, D_MODEL)
```

```python
import functools

import jax
import jax.numpy as jnp
import numpy as np
from jax import lax
from jax.experimental import pallas as pl
from jax.experimental.pallas import tpu as pltpu

EPS = 1e-6
D_MODEL = 1024
D_FF = 2816
S5_GROUP = 16
S5_GROUPS = 64
S5_STATE = 64
D_INNER = 2048
HEADDIM = 64
HEADS = 32
SSD_GROUPS = 8
HPG = 4
SSD_STATE = 128
CONV_K = 4
CONV_DIM = 4096
BC_WIDTH = SSD_GROUPS * SSD_STATE
GROUP_COLS = HPG * HEADDIM

LANES = 128
SUBLANES = 8
S5_BLOCK_GROUPS = LANES // S5_GROUP
S5_BLOCKS = S5_GROUPS // S5_BLOCK_GROUPS
S5_BLOCK_STATE = S5_BLOCK_GROUPS * S5_STATE
SSD_CHUNK = 64
DT_PAD = LANES
MAIN_COLS = D_MODEL + D_INNER + CONV_DIM

VMEM_LIMIT = 56 * 1024 * 1024

F32 = jnp.float32
BF16 = jnp.bfloat16


def _dot(a, b):
    return jnp.dot(a, b, preferred_element_type=F32)


def _sigmoid(v):
    return 1.0 / (1.0 + jnp.exp(-v))


def _silu(v):
    return v * _sigmoid(v)


def _rms_scale(xf):
    return lax.rsqrt(jnp.mean(xf * xf, axis=-1, keepdims=True) + EPS)


def _split3(v):
    hi = v.astype(BF16)
    r1 = v - hi.astype(F32)
    mid = r1.astype(BF16)
    lo = (r1 - mid.astype(F32)).astype(BF16)
    return hi, mid, lo


def _dot_exact_rhs(v, m_bf16):
    hi, mid, lo = _split3(v)
    return _dot(hi, m_bf16) + _dot(mid, m_bf16) + _dot(lo, m_bf16)


def _whole():
    return pl.BlockSpec(memory_space=pltpu.VMEM)


def _params(sem):
    return pltpu.CompilerParams(dimension_semantics=sem, vmem_limit_bytes=VMEM_LIMIT)


def _ffn_body(x_ref, g_ref, wg_ref, wu_ref, wd_ref, fin_ref, o_ref, *, n_chunks, final):
    x = x_ref[...]
    h = (x * _rms_scale(x) * g_ref[...]).astype(BF16)
    width = D_FF // n_chunks
    acc = jnp.zeros(x.shape, F32)
    for c in range(n_chunks):
        sl = slice(c * width, (c + 1) * width)
        a = _dot(h, wg_ref[:, sl])
        b = _dot(h, wu_ref[:, sl])
        act = (_silu(a) * b).astype(BF16)
        acc = acc + _dot(act, wd_ref[sl, :])
    y = x + 0.5 * acc
    if final:
        y = y * _rms_scale(y) * fin_ref[...]
    o_ref[...] = y


def _ffn(x, g, wg, wu, wd, fin, *, final, tm=512):
    t = x.shape[0]
    tm = min(tm, t)
    tile = pl.BlockSpec((tm, D_MODEL), lambda i: (i, 0))
    return pl.pallas_call(
        functools.partial(_ffn_body, n_chunks=2, final=final),
        grid=(t // tm,),
        in_specs=[tile, _whole(), _whole(), _whole(), _whole(), _whole()],
        out_specs=tile,
        out_shape=jax.ShapeDtypeStruct((t, D_MODEL), F32),
        compiler_params=_params(("parallel",)),
        name="ffn_final" if final else "ffn",
    )(x, g, wg, wu, wd, fin)


def _inproj_body(x_ref, g_ref, wmain_ref, wdt_ref, wgt_ref, bg_ref, cw_ref, cb_ref, dtb_ref,
                 u_ref, zs_ref, xbc_ref, dt_ref, gate_ref, buf_ref):
    tm = x_ref.shape[0]
    i = pl.program_id(1)
    x = x_ref[...]
    h = (x * _rms_scale(x) * g_ref[...]).astype(BF16)
    u_ref[...] = _dot(h, wmain_ref[:, :D_MODEL]).astype(BF16)
    zs_ref[...] = _silu(_dot(h, wmain_ref[:, D_MODEL:D_MODEL + D_INNER])).astype(BF16)

    @pl.when(i == 0)
    def _():
        buf_ref[0:SUBLANES, :] = jnp.zeros((SUBLANES, CONV_DIM), F32)

    @pl.when(i > 0)
    def _():
        buf_ref[0:SUBLANES, :] = buf_ref[tm:tm + SUBLANES, :]

    buf_ref[SUBLANES:SUBLANES + tm, :] = _dot(h, wmain_ref[:, D_MODEL + D_INNER:])
    full = buf_ref[...]
    acc = cb_ref[...] + cw_ref[CONV_K - 1:CONV_K, :] * full[SUBLANES:, :]
    for j in range(1, CONV_K):
        acc = acc + cw_ref[CONV_K - 1 - j:CONV_K - j, :] * pltpu.roll(full, j, axis=0)[SUBLANES:, :]
    xbc_ref[...] = _silu(acc).astype(BF16)

    raw = _dot(h, wdt_ref[...]) + dtb_ref[...]
    dt_ref[...] = jnp.maximum(raw, 0.0) + jnp.log1p(jnp.exp(-jnp.abs(raw)))
    gate_ref[...] = _sigmoid(_dot(h, wgt_ref[...]) + bg_ref[...]).astype(BF16)


def _inproj(x, g, wmain, wdt, wgt, bg, cw, cb, dtb, *, batch, tm=256):
    t = x.shape[0]
    seq = t // batch
    tm = min(tm, seq)
    nt = seq // tm

    def tile(c):
        return pl.BlockSpec((tm, c), lambda b, i: (b * nt + i, 0))

    outs = ((D_MODEL, BF16), (D_INNER, BF16), (CONV_DIM, BF16), (DT_PAD, F32), (2 * D_MODEL, BF16))
    return pl.pallas_call(
        _inproj_body,
        grid=(batch, nt),
        in_specs=[tile(D_MODEL)] + [_whole()] * 8,
        out_specs=[tile(c) for c, _ in outs],
        out_shape=[jax.ShapeDtypeStruct((t, c), d) for c, d in outs],
        scratch_shapes=[pltpu.VMEM((tm + SUBLANES, CONV_DIM), F32)],
        compiler_params=_params(("parallel", "arbitrary")),
        name="inproj",
    )(x, g, wmain, wdt, wgt, bg, cw, cb, dtb)


def _s5_body(u_ref, w1_ref, cbd_ref, apow_ref, y_ref, ubuf_ref, bp_ref, x_ref, ybuf_ref, *, n_sub):
    bsz, tt, _ = u_ref.shape
    m = bsz * tt
    i = pl.program_id(1)

    @pl.when(i == 0)
    def _():
        ubuf_ref[0:SUBLANES, :] = jnp.zeros((SUBLANES, LANES), F32)
        x_ref[...] = jnp.zeros(x_ref.shape, F32)

    @pl.when(i > 0)
    def _():
        ubuf_ref[0:SUBLANES, :] = ubuf_ref[m:m + SUBLANES, :]

    for b in range(bsz):
        ubuf_ref[pl.ds(SUBLANES + b, tt, stride=bsz), :] = u_ref[b].astype(F32)
    ucat = jnp.concatenate(
        [ubuf_ref[SUBLANES - k * bsz:SUBLANES - k * bsz + m, :].astype(BF16) for k in range(n_sub)], axis=1)
    bp_ref[...] = _dot(ucat, w1_ref[0])

    ns = S5_BLOCK_STATE
    a_r = jnp.broadcast_to(apow_ref[0, 0:1, :], (SUBLANES, ns))
    a_i = jnp.broadcast_to(apow_ref[0, 1:2, :], (SUBLANES, ns))

    def step(k, carry):
        xr, xi = carry
        r = pl.multiple_of(k * SUBLANES, SUBLANES)
        nr = a_r * xr - a_i * xi + bp_ref[pl.ds(r, SUBLANES), 0:ns]
        ni = a_r * xi + a_i * xr + bp_ref[pl.ds(r, SUBLANES), ns:2 * ns]
        bp_ref[pl.ds(r, SUBLANES), 0:ns] = nr
        bp_ref[pl.ds(r, SUBLANES), ns:2 * ns] = ni
        return nr, ni

    xr, xi = lax.fori_loop(0, m // SUBLANES, step, (x_ref[:, 0:ns], x_ref[:, ns:2 * ns]), unroll=2)
    x_ref[:, 0:ns] = xr
    x_ref[:, ns:2 * ns] = xi

    ybuf_ref[...] = _dot(bp_ref[...].astype(BF16), cbd_ref[0])
    for b in range(bsz):
        y_ref[b] = ybuf_ref[pl.ds(b, tt, stride=bsz), :]


def _s5(u3, w1, cbd, apow, *, tt=512):
    bsz, seq, _ = u3.shape
    n_sub = SUBLANES // bsz
    tt = min(tt, seq)
    m = bsz * tt
    blk = pl.BlockSpec((bsz, tt, LANES), lambda j, i: (0, i, j))
    return pl.pallas_call(
        functools.partial(_s5_body, n_sub=n_sub),
        grid=(S5_BLOCKS, seq // tt),
        in_specs=[blk,
                  pl.BlockSpec((1, n_sub * LANES, 2 * S5_BLOCK_STATE), lambda j, i: (j, 0, 0)),
                  pl.BlockSpec((1, 2 * S5_BLOCK_STATE, LANES), lambda j, i: (j, 0, 0)),
                  pl.BlockSpec((1, 2, S5_BLOCK_STATE), lambda j, i: (j, 0, 0))],
        out_specs=blk,
        out_shape=jax.ShapeDtypeStruct((bsz, seq, D_MODEL), F32),
        scratch_shapes=[pltpu.VMEM((m + SUBLANES, LANES), F32),
                        pltpu.VMEM((m, 2 * S5_BLOCK_STATE), F32),
                        pltpu.VMEM((SUBLANES, 2 * S5_BLOCK_STATE), F32),
                        pltpu.VMEM((m, LANES), F32)],
        compiler_params=_params(("parallel", "arbitrary")),
        name="s5",
    )(u3, w1, cbd, apow)


def _s5_operators(a_re, a_im, log_dt, b_re, b_im, c_re, c_im, n_sub):
    dt = jnp.exp(log_dt)[:, None]
    lr, li = a_re, a_im
    mag = jnp.exp(lr * dt)
    ar, ai = mag * jnp.cos(li * dt), mag * jnp.sin(li * dt)
    den = lr * lr + li * li
    cr = ((ar - 1.0) * lr + ai * li) / den
    ci = (ai * lr - (ar - 1.0) * li) / den
    bbr = cr[..., None] * b_re - ci[..., None] * b_im
    bbi = cr[..., None] * b_im + ci[..., None] * b_re

    def power(k):
        m = jnp.exp(k * lr * dt)
        return m * jnp.cos(k * li * dt), m * jnp.sin(k * li * dt)

    eye = jnp.eye(S5_BLOCK_GROUPS, dtype=F32)

    def blockdiag_in(w):
        w = w.reshape(S5_BLOCKS, S5_BLOCK_GROUPS, S5_STATE, S5_GROUP)
        return jnp.einsum('jgnm,gh->jgmhn', w, eye).reshape(S5_BLOCKS, LANES, S5_BLOCK_STATE)

    def blockdiag_out(c):
        c = c.reshape(S5_BLOCKS, S5_BLOCK_GROUPS, S5_GROUP, S5_STATE)
        return jnp.einsum('jgmn,gh->jgnhm', c, eye).reshape(S5_BLOCKS, S5_BLOCK_STATE, LANES)

    rows = []
    for k in range(n_sub):
        pr, pi = power(float(k))
        wr = pr[..., None] * bbr - pi[..., None] * bbi
        wi = pr[..., None] * bbi + pi[..., None] * bbr
        rows.append(jnp.concatenate([blockdiag_in(wr), blockdiag_in(wi)], axis=-1))
    w1 = jnp.concatenate(rows, axis=1).astype(BF16)
    cbd = jnp.concatenate([blockdiag_out(c_re), blockdiag_out(-c_im)], axis=1).astype(BF16)
    sr, si = power(float(n_sub))
    apow = jnp.stack([sr.reshape(S5_BLOCKS, S5_BLOCK_STATE), si.reshape(S5_BLOCKS, S5_BLOCK_STATE)], axis=1)
    return w1, cbd, apow


def _ssd_chunk(xbc, zs, dt_full, alog_ref, dexp_ref, nw_ref, e_ref, tri_ref, h_ref):
    lc = SSD_CHUNK
    xs_b = xbc[:, :D_INNER]
    xs = xs_b.astype(F32)
    bm = xbc[:, D_INNER:D_INNER + BC_WIDTH]
    cm = xbc[:, D_INNER + BC_WIDTH:]

    dt = dt_full[:, :HEADS]
    da = dt * (-jnp.exp(alog_ref[...]))
    hi, mid, lo = _split3(da)
    tri = tri_ref[...]
    a_cum = _dot(tri, hi) + _dot(tri, mid) + _dot(tri, lo)

    e = e_ref[...]
    a_e = _dot_exact_rhs(a_cum, e)
    dt_e = _dot_exact_rhs(dt, e)
    row = lax.broadcasted_iota(jnp.int32, (lc, D_INNER), 0)
    lane_j = lax.broadcasted_iota(jnp.int32, (lc, D_INNER), 1) % lc
    diag = row == lane_j
    a_j = jnp.sum(jnp.where(diag, a_e, 0.0), axis=0, keepdims=True)
    dt_j = jnp.sum(jnp.where(diag, dt_e, 0.0), axis=0, keepdims=True)
    lmat = jnp.exp(jnp.where(lane_j <= row, a_e - a_j, -jnp.inf))
    a_last = a_e[lc - 1:lc, :]
    exp_a = jnp.exp(a_e)
    xw = (xs * (jnp.exp(a_last - a_e) * dt_e)).astype(BF16)
    chunk_decay = jnp.exp(a_last)

    blk_r = lax.broadcasted_iota(jnp.int32, (GROUP_COLS, GROUP_COLS), 0) // HEADDIM
    blk_c = lax.broadcasted_iota(jnp.int32, (GROUP_COLS, GROUP_COLS), 1) // HEADDIM
    same_head = blk_r == blk_c
    nt = (((1,), (1,)), ((), ()))
    tn = (((0,), (0,)), ((), ()))
    ys = []
    for g in range(SSD_GROUPS):
        cs = slice(g * GROUP_COLS, (g + 1) * GROUP_COLS)
        ns = slice(g * SSD_STATE, (g + 1) * SSD_STATE)
        bm_g, cm_g = bm[:, ns], cm[:, ns]
        cb = lax.dot_general(cm_g, jnp.concatenate([bm_g] * HPG, axis=0), nt,
                             preferred_element_type=F32)
        w = (cb * lmat[:, cs] * dt_j[:, cs]).astype(BF16)
        x_g = xs_b[:, cs]
        x_bd = jnp.where(same_head, jnp.concatenate([x_g] * HPG, axis=0), jnp.zeros((), BF16))
        y_diag = _dot(w, x_bd)
        h_prev = h_ref[g]
        y_off = _dot(cm_g, h_prev.astype(BF16)) * exp_a[:, cs]
        ys.append(y_diag + y_off)
        st = lax.dot_general(bm_g, xw[:, cs], tn, preferred_element_type=F32)
        h_ref[g] = h_prev * chunk_decay[:, cs] + st

    y = jnp.concatenate(ys, axis=1) + dexp_ref[...] * xs
    t = y * zs.astype(F32)
    return (t * _rms_scale(t) * nw_ref[...]).astype(BF16)


def _ssd_body(xbc_ref, zs_ref, dt_ref, alog_ref, dexp_ref, nw_ref, e_ref, tri_ref, y_ref, h_ref, *, cps):
    lc = SSD_CHUNK

    @pl.when(pl.program_id(1) == 0)
    def _():
        h_ref[...] = jnp.zeros(h_ref.shape, F32)

    for ci in range(cps):
        rs = slice(ci * lc, (ci + 1) * lc)
        y_ref[rs, :] = _ssd_chunk(xbc_ref[rs, :], zs_ref[rs, :], dt_ref[rs, :], alog_ref, dexp_ref, nw_ref,
                                  e_ref, tri_ref, h_ref)


def _ssd(xbc, zs, dt, alog, dexp, nw, e, tri, *, batch, cps=2):
    t = xbc.shape[0]
    seq = t // batch
    cps = min(cps, seq // SSD_CHUNK)
    rows = cps * SSD_CHUNK
    ns = seq // rows

    def tile(c):
        return pl.BlockSpec((rows, c), lambda b, i: (b * ns + i, 0))

    return pl.pallas_call(
        functools.partial(_ssd_body, cps=cps),
        grid=(batch, ns),
        in_specs=[tile(CONV_DIM), tile(D_INNER), tile(DT_PAD)] + [_whole()] * 5,
        out_specs=tile(D_INNER),
        out_shape=jax.ShapeDtypeStruct((t, D_INNER), BF16),
        scratch_shapes=[pltpu.VMEM((SSD_GROUPS, SSD_STATE, GROUP_COLS), F32)],
        compiler_params=_params(("parallel", "arbitrary")),
        name="ssd",
    )(xbc, zs, dt, alog, dexp, nw, e, tri)


def _merge_body(x_ref, y5_ref, u_ref, yssd_ref, gate_ref, dskip_ref, wglu_ref, bglu_ref,
                wp5_ref, wpssd_ref, wout_ref, o_ref):
    y = y5_ref[...] + dskip_ref[...] * u_ref[...].astype(F32)
    gl = 0.5 * y * (1.0 + jnp.tanh(np.float32(np.sqrt(2.0 / np.pi)) * (y + 0.044715 * (y * y * y))))
    glu = gl * _sigmoid(_dot(gl.astype(BF16), wglu_ref[...]) + bglu_ref[...])
    p5 = _dot(glu.astype(BF16), wp5_ref[...])
    pssd = _dot(yssd_ref[...], wpssd_ref[...])
    merged = gate_ref[:, :D_MODEL].astype(F32) * p5 + gate_ref[:, D_MODEL:].astype(F32) * pssd
    o_ref[...] = x_ref[...] + _dot(merged.astype(BF16), wout_ref[...])


def _merge(x, y5, u, yssd, gates, dskip, wglu, bglu, wp5, wpssd, wout, *, tm=256):
    t = x.shape[0]
    tm = min(tm, t)

    def tile(c):
        return pl.BlockSpec((tm, c), lambda i: (i, 0))

    return pl.pallas_call(
        _merge_body,
        grid=(t // tm,),
        in_specs=[tile(D_MODEL), tile(D_MODEL), tile(D_MODEL), tile(D_INNER), tile(2 * D_MODEL)]
        + [_whole()] * 6,
        out_specs=tile(D_MODEL),
        out_shape=jax.ShapeDtypeStruct((t, D_MODEL), F32),
        compiler_params=_params(("parallel",)),
        name="merge",
    )(x, y5, u, yssd, gates, dskip, wglu, bglu, wp5, wpssd, wout)


def kernel(x, ffn1_norm, ffn1_w_gate, ffn1_w_up, ffn1_w_down, mix_norm, w_in, conv_w, conv_b, s5_A_re, s5_A_im, s5_log_dt, s5_B_re, s5_B_im, s5_C_re, s5_C_im, s5_D, s5_w_glu, s5_b_glu, ssd_A_log, ssd_dt_bias, ssd_D, ssd_norm, w_proj_s5, w_proj_ssd, b_gate, w_out, ffn2_norm, ffn2_w_gate, ffn2_w_up, ffn2_w_down, final_norm):
    bsz, seq, _ = x.shape
    assert SUBLANES % bsz == 0, "the S5 row interleave needs the batch to divide the sublane count"
    t = bsz * seq
    depth = ffn1_norm.shape[0]
    xt = x.reshape(t, D_MODEL)
    row = lambda v: v.reshape(1, -1).astype(F32)
    fin = row(final_norm)

    e_mat = jnp.asarray(np.repeat(np.eye(HEADS, dtype=np.float32), HEADDIM, axis=1), BF16)
    tri = jnp.asarray(np.tril(np.ones((SSD_CHUNK, SSD_CHUNK), np.float32)), BF16)
    dt_end = MAIN_COLS + HEADS

    for i in range(depth):
        xt = _ffn(xt, row(ffn1_norm[i]), ffn1_w_gate[i].astype(BF16), ffn1_w_up[i].astype(BF16),
                  ffn1_w_down[i].astype(BF16), fin, final=False)

        w = w_in[i]
        w_dt = jnp.pad(w[:, MAIN_COLS:dt_end], ((0, 0), (0, DT_PAD - HEADS))).astype(BF16)
        dtb = jnp.pad(ssd_dt_bias[i], (0, DT_PAD - HEADS)).reshape(1, DT_PAD)
        u, zs, xbc, dt, gates = _inproj(
            xt, row(mix_norm[i]), w[:, :MAIN_COLS].astype(BF16), w_dt, w[:, dt_end:].astype(BF16),
            row(b_gate[i]), conv_w[i], row(conv_b[i]), dtb, batch=bsz)

        w1, cbd, apow = _s5_operators(s5_A_re[i], s5_A_im[i], s5_log_dt[i], s5_B_re[i], s5_B_im[i],
                                      s5_C_re[i], s5_C_im[i], SUBLANES // bsz)
        y5 = _s5(u.reshape(bsz, seq, D_MODEL), w1, cbd, apow).reshape(t, D_MODEL)

        yssd = _ssd(xbc, zs, dt, row(ssd_A_log[i]), row(jnp.repeat(ssd_D[i], HEADDIM)), row(ssd_norm[i]),
                    e_mat, tri, batch=bsz)

        xt = _merge(xt, y5, u, yssd, gates, row(s5_D[i]), s5_w_glu[i].astype(BF16), row(s5_b_glu[i]),
                    w_proj_s5[i].astype(BF16), w_proj_ssd[i].astype(BF16), w_out[i].astype(BF16))

        xt = _ffn(xt, row(ffn2_norm[i]), ffn2_w_gate[i].astype(BF16), ffn2_w_up[i].astype(BF16),
                  ffn2_w_down[i].astype(BF16), fin, final=(i == depth - 1))
    return xt.reshape(bsz, seq, D_MODEL)
```

```python
import functools

import jax
import jax.numpy as jnp
import numpy as np
from jax import lax
from jax.experimental import pallas as pl
from jax.experimental.pallas import tpu as pltpu

EPS = 1e-6
D_MODEL = 1024
D_FF = 2816
S5_GROUP = 16
S5_GROUPS = 64
S5_STATE = 64
D_INNER = 2048
HEADDIM = 64
HEADS = 32
SSD_GROUPS = 8
HPG = 4
SSD_STATE = 128
CONV_K = 4
CONV_DIM = 4096
BC_WIDTH = SSD_GROUPS * SSD_STATE
GROUP_COLS = HPG * HEADDIM

LANES = 128
SUBLANES = 8
ROW_BLOCK = SUBLANES * SUBLANES
S5_BLOCK_GROUPS = LANES // S5_GROUP
S5_BLOCKS = S5_GROUPS // S5_BLOCK_GROUPS
S5_BLOCK_STATE = S5_BLOCK_GROUPS * S5_STATE
S5_PAIR = 2
SSD_CHUNK = ROW_BLOCK
DT_PAD = LANES
PROJ_COLS = 512
FF_COLS = 256
MAIN_COLS = D_MODEL + D_INNER + CONV_DIM

VMEM_LIMIT = 56 * 1024 * 1024

F32 = jnp.float32
BF16 = jnp.bfloat16


def _dot(a, b):
    return jnp.dot(a, b, preferred_element_type=F32)


def _sigmoid(v):
    return 1.0 / (1.0 + jnp.exp(-v))


def _silu(v):
    return v * _sigmoid(v)


def _rms_scale(xf):
    return lax.rsqrt(jnp.mean(xf * xf, axis=-1, keepdims=True) + EPS)


def _split3(v):
    hi = v.astype(BF16)
    r1 = v - hi.astype(F32)
    mid = r1.astype(BF16)
    lo = (r1 - mid.astype(F32)).astype(BF16)
    return hi, mid, lo


def _dot_exact_rhs(v, m_bf16):
    hi, mid, lo = _split3(v)
    return _dot(hi, m_bf16) + _dot(mid, m_bf16) + _dot(lo, m_bf16)


def _load_transposed(ref, scr_ref):
    nlb = ref.shape[1] // LANES
    for j in range(nlb):
        scr_ref[j] = ref[:, j * LANES:(j + 1) * LANES]
    parts = [jnp.concatenate([scr_ref[j, pl.ds(c * ROW_BLOCK + k, SUBLANES, stride=SUBLANES), :]
                              for j in range(nlb)], axis=1)
             for c in range(ref.shape[0] // ROW_BLOCK) for k in range(SUBLANES)]
    return jnp.concatenate(parts, axis=0)


def _store_transposed(ref, val, scr_ref, c0=0):
    nlb = val.shape[1] // LANES
    for j in range(nlb):
        for c in range(val.shape[0] // ROW_BLOCK):
            for k in range(SUBLANES):
                r = c * ROW_BLOCK + k * SUBLANES
                scr_ref[j, pl.ds(c * ROW_BLOCK + k, SUBLANES, stride=SUBLANES), :] = (
                    val[r:r + SUBLANES, j * LANES:(j + 1) * LANES])
    for j in range(nlb):
        ref[:, c0 + j * LANES:c0 + (j + 1) * LANES] = scr_ref[j].astype(ref.dtype)


def _whole():
    return pl.BlockSpec(memory_space=pltpu.VMEM)


def _params(sem):
    return pltpu.CompilerParams(dimension_semantics=sem, vmem_limit_bytes=VMEM_LIMIT)


def _ffn_body(x_ref, g_ref, wg_ref, wu_ref, wd_ref, fin_ref, o_ref, scr_ref, *, first, final):
    x = _load_transposed(x_ref, scr_ref) if first else x_ref[...]
    h = (x * _rms_scale(x) * g_ref[...]).astype(BF16)
    acc = jnp.zeros(x.shape, F32)
    for c0 in range(0, D_FF, FF_COLS):
        sl = slice(c0, c0 + FF_COLS)
        a = _dot(h, wg_ref[:, sl])
        b = _dot(h, wu_ref[:, sl])
        act = (_silu(a) * b).astype(BF16)
        acc = acc + _dot(act, wd_ref[sl, :])
    y = x + 0.5 * acc
    if final:
        y = y * _rms_scale(y) * fin_ref[...]
        _store_transposed(o_ref, y, scr_ref)
    else:
        o_ref[...] = y


def _ffn(x, g, wg, wu, wd, fin, *, first, final, tm=512):
    t = x.shape[0]
    tm = min(tm, t)
    tile = pl.BlockSpec((tm, D_MODEL), lambda i: (i, 0))
    return pl.pallas_call(
        functools.partial(_ffn_body, first=first, final=final),
        grid=(t // tm,),
        in_specs=[tile, _whole(), _whole(), _whole(), _whole(), _whole()],
        out_specs=tile,
        out_shape=jax.ShapeDtypeStruct((t, D_MODEL), F32),
        scratch_shapes=[pltpu.VMEM((D_MODEL // LANES, tm, LANES), F32)],
        compiler_params=_params(("parallel",)),
        name="ffn_final" if final else "ffn",
    )(x, g, wg, wu, wd, fin)


def _inproj_body(x_ref, g_ref, wmain_ref, wdt_ref, wgt_ref, cw_ref, cb_ref, dtb_ref,
                 u_ref, z_ref, xbc_ref, dt_ref, gl_ref, tail_ref, scr_ref):
    tm = x_ref.shape[0]
    nblk = tm // ROW_BLOCK
    x = x_ref[...]
    h = (x * _rms_scale(x) * g_ref[...]).astype(BF16)
    for n, c0 in enumerate(range(0, D_MODEL, PROJ_COLS)):
        _store_transposed(u_ref, _dot(h, wmain_ref[:, c0:c0 + PROJ_COLS]), scr_ref.at[n % 2], c0)
    for c0 in range(0, D_INNER, PROJ_COLS):
        z_ref[:, c0:c0 + PROJ_COLS] = _dot(h, wmain_ref[:, D_MODEL + c0:D_MODEL + c0 + PROJ_COLS]).astype(BF16)
    for c0 in range(0, 2 * D_MODEL, PROJ_COLS):
        gl_ref[:, c0:c0 + PROJ_COLS] = _dot(h, wgt_ref[:, c0:c0 + PROJ_COLS]).astype(BF16)
    raw = _dot(h, wdt_ref[...]) + dtb_ref[...]
    dt_ref[...] = jnp.maximum(raw, 0.0) + jnp.log1p(jnp.exp(-jnp.abs(raw)))

    @pl.when(pl.program_id(1) == 0)
    def _():
        tail_ref[...] = jnp.zeros(tail_ref.shape, F32)

    nd = CONV_K - 1
    sub = lax.broadcasted_iota(jnp.int32, (nblk, nd, SUBLANES, PROJ_COLS), 2)
    base = D_MODEL + D_INNER
    for c0 in range(0, CONV_DIM, PROJ_COLS):
        cs = slice(c0, c0 + PROJ_COLS)
        v = _dot(h, wmain_ref[:, base + c0:base + c0 + PROJ_COLS]).reshape(nblk, SUBLANES, SUBLANES, PROJ_COLS)
        rolled = pltpu.roll(v[:, SUBLANES - nd:].reshape(nblk * nd, SUBLANES, PROJ_COLS), 1, axis=1)
        rolled = rolled.reshape(nblk, nd, SUBLANES, PROJ_COLS)
        carry = tail_ref[:, cs].reshape(1, nd, SUBLANES, PROJ_COLS)
        tail_ref[:, cs] = rolled[nblk - 1].reshape(nd * SUBLANES, PROJ_COLS)
        prev = jnp.concatenate([carry, rolled[:nblk - 1]], axis=0) if nblk > 1 else carry
        edge = jnp.where(sub >= 1, rolled, prev)
        w_tap = lambda k: cw_ref[k:k + 1, cs].reshape(1, 1, 1, PROJ_COLS)
        acc = cb_ref[:, cs].reshape(1, 1, 1, PROJ_COLS) + w_tap(CONV_K - 1) * v
        for j in range(1, CONV_K):
            delayed = jnp.concatenate([edge[:, nd - j:], v[:, :SUBLANES - j]], axis=1)
            acc = acc + w_tap(CONV_K - 1 - j) * delayed
        xbc_ref[:, cs] = _silu(acc).reshape(tm, PROJ_COLS).astype(BF16)


def _inproj(x, g, wmain, wdt, wgt, cw, cb, dtb, *, batch, tm=256):
    t = x.shape[0]
    seq = t // batch
    tm = min(tm, seq)
    nt = seq // tm

    def tile(c):
        return pl.BlockSpec((tm, c), lambda b, i: (b * nt + i, 0))

    outs = ((D_MODEL, BF16), (D_INNER, BF16), (CONV_DIM, BF16), (DT_PAD, F32), (2 * D_MODEL, BF16))
    return pl.pallas_call(
        _inproj_body,
        grid=(batch, nt),
        in_specs=[tile(D_MODEL)] + [_whole()] * 7,
        out_specs=[tile(c) for c, _ in outs],
        out_shape=[jax.ShapeDtypeStruct((t, c), d) for c, d in outs],
        scratch_shapes=[pltpu.VMEM(((CONV_K - 1) * SUBLANES, CONV_DIM), F32),
                        pltpu.VMEM((2, PROJ_COLS // LANES, tm, LANES), F32)],
        compiler_params=_params(("parallel", "arbitrary")),
        name="inproj",
    )(x, g, wmain, wdt, wgt, cw, cb, dtb)


def _s5_body(u_ref, w1_ref, cbd_ref, apow_ref, dskip_ref, y_ref, ubuf_ref, bp_ref, xs_ref, x_ref, ybuf_ref,
             *, n_sub):
    bsz, tt, _ = u_ref.shape
    m = bsz * tt
    ns = S5_PAIR * S5_BLOCK_STATE
    i = pl.program_id(1)

    @pl.when(i == 0)
    def _():
        ubuf_ref[:, 0:SUBLANES, :] = jnp.zeros((S5_PAIR, SUBLANES, LANES), F32)
        x_ref[...] = jnp.zeros(x_ref.shape, F32)

    @pl.when(i > 0)
    def _():
        ubuf_ref[:, 0:SUBLANES, :] = ubuf_ref[:, m:m + SUBLANES, :]

    for q in range(S5_PAIR):
        for b in range(bsz):
            ubuf_ref[q, pl.ds(SUBLANES + b, tt, stride=bsz), :] = u_ref[b, :, q * LANES:(q + 1) * LANES].astype(F32)
        ucat = jnp.concatenate(
            [ubuf_ref[q, SUBLANES - k * bsz:SUBLANES - k * bsz + m, :].astype(BF16) for k in range(n_sub)], axis=1)
        bp_ref[:, q * S5_BLOCK_STATE:(q + 1) * S5_BLOCK_STATE] = _dot(ucat, w1_ref[q, 0])
        bp_ref[:, ns + q * S5_BLOCK_STATE:ns + (q + 1) * S5_BLOCK_STATE] = _dot(ucat, w1_ref[q, 1])

    a_r = jnp.broadcast_to(apow_ref[0, 0:1, :], (SUBLANES, ns))
    a_i = jnp.broadcast_to(apow_ref[0, 1:2, :], (SUBLANES, ns))
    pair = 2 * SUBLANES

    def step(k, carry):
        xr, xi = carry
        r = pl.multiple_of(k * pair, pair)
        r0 = a_r * xr - a_i * xi + bp_ref[pl.ds(r, SUBLANES), 0:ns]
        i0 = a_r * xi + a_i * xr + bp_ref[pl.ds(r, SUBLANES), ns:2 * ns]
        r1 = a_r * r0 - a_i * i0 + bp_ref[pl.ds(r + SUBLANES, SUBLANES), 0:ns]
        i1 = a_r * i0 + a_i * r0 + bp_ref[pl.ds(r + SUBLANES, SUBLANES), ns:2 * ns]
        xs_ref[pl.ds(r, pair), 0:ns] = jnp.concatenate([r0, r1], axis=0).astype(BF16)
        xs_ref[pl.ds(r, pair), ns:2 * ns] = jnp.concatenate([i0, i1], axis=0).astype(BF16)
        return r1, i1

    xr, xi = lax.fori_loop(0, m // pair, step, (x_ref[:, 0:ns], x_ref[:, ns:2 * ns]))
    x_ref[:, 0:ns] = xr
    x_ref[:, ns:2 * ns] = xi

    y = _dot(xs_ref[...], cbd_ref[0])
    for q in range(S5_PAIR):
        cols = slice(q * LANES, (q + 1) * LANES)
        ybuf_ref[q] = y[:, cols] + dskip_ref[:, cols] * ubuf_ref[q, SUBLANES:SUBLANES + m, :]
        for b in range(bsz):
            y_ref[b, :, cols] = ybuf_ref[q, pl.ds(b, tt, stride=bsz), :]


def _s5(u3, w1, cbd, apow, dskip, *, tt=512):
    bsz, seq, _ = u3.shape
    n_sub = SUBLANES // bsz
    tt = min(tt, seq)
    m = bsz * tt
    width = S5_PAIR * LANES
    ns = S5_PAIR * S5_BLOCK_STATE
    blk = pl.BlockSpec((bsz, tt, width), lambda j, i: (0, i, j))
    return pl.pallas_call(
        functools.partial(_s5_body, n_sub=n_sub),
        grid=(S5_BLOCKS // S5_PAIR, seq // tt),
        in_specs=[blk,
                  pl.BlockSpec((S5_PAIR, 2, n_sub * LANES, S5_BLOCK_STATE), lambda j, i: (j, 0, 0, 0)),
                  pl.BlockSpec((1, 2 * ns, width), lambda j, i: (j, 0, 0)),
                  pl.BlockSpec((1, 2, ns), lambda j, i: (j, 0, 0)),
                  pl.BlockSpec((1, width), lambda j, i: (0, j))],
        out_specs=blk,
        out_shape=jax.ShapeDtypeStruct((bsz, seq, D_MODEL), F32),
        scratch_shapes=[pltpu.VMEM((S5_PAIR, m + SUBLANES, LANES), F32),
                        pltpu.VMEM((m, 2 * ns), F32),
                        pltpu.VMEM((m, 2 * ns), BF16),
                        pltpu.VMEM((SUBLANES, 2 * ns), F32),
                        pltpu.VMEM((S5_PAIR, m, LANES), F32)],
        compiler_params=_params(("parallel", "arbitrary")),
        name="s5",
    )(u3, w1, cbd, apow, dskip)


def _s5_operators(a_re, a_im, log_dt, b_re, b_im, c_re, c_im, n_sub):
    dt = jnp.exp(log_dt)[:, None]
    lr, li = a_re, a_im
    mag = jnp.exp(lr * dt)
    ar, ai = mag * jnp.cos(li * dt), mag * jnp.sin(li * dt)
    den = lr * lr + li * li
    cr = ((ar - 1.0) * lr + ai * li) / den
    ci = (ai * lr - (ar - 1.0) * li) / den
    bbr = cr[..., None] * b_re - ci[..., None] * b_im
    bbi = cr[..., None] * b_im + ci[..., None] * b_re

    def power(k):
        m = jnp.exp(k * lr * dt)
        return m * jnp.cos(k * li * dt), m * jnp.sin(k * li * dt)

    eye = jnp.eye(S5_BLOCK_GROUPS, dtype=F32)

    def blockdiag_in(w):
        w = w.reshape(S5_BLOCKS, S5_BLOCK_GROUPS, S5_STATE, S5_GROUP)
        return jnp.einsum('jgnm,gh->jgmhn', w, eye).reshape(S5_BLOCKS, LANES, S5_BLOCK_STATE)

    def blockdiag_out(c):
        c = c.reshape(S5_BLOCKS, S5_BLOCK_GROUPS, S5_GROUP, S5_STATE)
        return jnp.einsum('jgmn,gh->jgnhm', c, eye).reshape(S5_BLOCKS, S5_BLOCK_STATE, LANES)

    rows = []
    for k in range(n_sub):
        pr, pi = power(float(k))
        wr = pr[..., None] * bbr - pi[..., None] * bbi
        wi = pr[..., None] * bbi + pi[..., None] * bbr
        rows.append(jnp.stack([blockdiag_in(wr), blockdiag_in(wi)], axis=1))
    w1 = jnp.concatenate(rows, axis=2).astype(BF16)
    eye_p = jnp.eye(S5_PAIR, dtype=F32)

    def pair_out(c):
        c = c.reshape(S5_BLOCKS // S5_PAIR, S5_PAIR, S5_BLOCK_STATE, LANES)
        return jnp.einsum('jqnm,qr->jqnrm', c, eye_p).reshape(S5_BLOCKS // S5_PAIR, S5_PAIR * S5_BLOCK_STATE,
                                                              S5_PAIR * LANES)

    cbd = jnp.concatenate([pair_out(blockdiag_out(c_re)), pair_out(blockdiag_out(-c_im))], axis=1).astype(BF16)
    sr, si = power(float(n_sub))
    apow = jnp.stack([sr.reshape(S5_BLOCKS // S5_PAIR, S5_PAIR * S5_BLOCK_STATE),
                      si.reshape(S5_BLOCKS // S5_PAIR, S5_PAIR * S5_BLOCK_STATE)], axis=1)
    return w1, cbd, apow


def _ssd_chunk(xbc, dt_full, alog_ref, dexp_ref, e_ref, tri_ref, causal_ref, diag_ref, h_ref):
    lc = SSD_CHUNK
    xs_b = xbc[:, :D_INNER]
    xs = xs_b.astype(F32)
    bm = xbc[:, D_INNER:D_INNER + BC_WIDTH]
    cm = xbc[:, D_INNER + BC_WIDTH:]

    dt = dt_full[:, :HEADS]
    da = dt * (-jnp.exp(alog_ref[...]))
    hi, mid, lo = _split3(da)
    tri = tri_ref[...]
    a_cum = _dot(tri, hi) + _dot(tri, mid) + _dot(tri, lo)

    e = e_ref[...]
    a_e = _dot_exact_rhs(a_cum, e)
    dt_e = _dot_exact_rhs(dt, e)
    diag = diag_ref[...]
    a_j = jnp.sum(a_e * diag, axis=0, keepdims=True)
    dt_j = jnp.sum(dt_e * diag, axis=0, keepdims=True)
    lmat = jnp.exp(a_e - a_j + causal_ref[...])
    a_last = a_e[lc - 1:lc, :]
    exp_a = jnp.exp(a_e)
    xw = (xs * (jnp.exp(a_last - a_e) * dt_e)).astype(BF16)
    chunk_decay = jnp.exp(a_last)

    blk_r = lax.broadcasted_iota(jnp.int32, (GROUP_COLS, GROUP_COLS), 0) // HEADDIM
    blk_c = lax.broadcasted_iota(jnp.int32, (GROUP_COLS, GROUP_COLS), 1) // HEADDIM
    same_head = blk_r == blk_c
    nt = (((1,), (1,)), ((), ()))
    tn = (((0,), (0,)), ((), ()))
    ys = []
    for g in range(SSD_GROUPS):
        cs = slice(g * GROUP_COLS, (g + 1) * GROUP_COLS)
        ns = slice(g * SSD_STATE, (g + 1) * SSD_STATE)
        bm_g, cm_g = bm[:, ns], cm[:, ns]
        cb = lax.dot_general(cm_g, jnp.concatenate([bm_g] * HPG, axis=0), nt,
                             preferred_element_type=F32)
        w = (cb * lmat[:, cs] * dt_j[:, cs]).astype(BF16)
        x_g = xs_b[:, cs]
        x_bd = jnp.where(same_head, jnp.concatenate([x_g] * HPG, axis=0), jnp.zeros((), BF16))
        y_diag = _dot(w, x_bd)
        h_prev = h_ref[g]
        y_off = _dot(cm_g, h_prev.astype(BF16)) * exp_a[:, cs]
        ys.append(y_diag + y_off)
        st = lax.dot_general(bm_g, xw[:, cs], tn, preferred_element_type=F32)
        h_ref[g] = h_prev * chunk_decay[:, cs] + st

    return (jnp.concatenate(ys, axis=1) + dexp_ref[...] * xs).astype(BF16)


def _ssd_body(xbc_ref, dt_ref, alog_ref, dexp_ref, e_ref, tri_ref, causal_ref, diag_ref, y_ref, h_ref, *, cps):
    lc = SSD_CHUNK

    @pl.when(pl.program_id(1) == 0)
    def _():
        h_ref[...] = jnp.zeros(h_ref.shape, F32)

    for ci in range(cps):
        rs = slice(ci * lc, (ci + 1) * lc)
        y_ref[rs, :] = _ssd_chunk(xbc_ref[rs, :], dt_ref[rs, :], alog_ref, dexp_ref, e_ref, tri_ref,
                                  causal_ref, diag_ref, h_ref)


def _ssd(xbc, dt, alog, dexp, e, tri, causal, diag, *, batch, cps=2):
    t = xbc.shape[0]
    seq = t // batch
    cps = min(cps, seq // SSD_CHUNK)
    rows = cps * SSD_CHUNK
    ns = seq // rows

    def tile(c):
        return pl.BlockSpec((rows, c), lambda b, i: (b * ns + i, 0))

    return pl.pallas_call(
        functools.partial(_ssd_body, cps=cps),
        grid=(batch, ns),
        in_specs=[tile(CONV_DIM), tile(DT_PAD)] + [_whole()] * 6,
        out_specs=tile(D_INNER),
        out_shape=jax.ShapeDtypeStruct((t, D_INNER), BF16),
        scratch_shapes=[pltpu.VMEM((SSD_GROUPS, SSD_STATE, GROUP_COLS), F32)],
        compiler_params=_params(("parallel", "arbitrary")),
        name="ssd",
    )(xbc, dt, alog, dexp, e, tri, causal, diag)


def _ssd_constants():
    r = np.arange(SSD_CHUNK)
    time = SUBLANES * (r % SUBLANES) + r // SUBLANES
    before = time[None, :] <= time[:, None]
    tri = before.astype(np.float32)
    causal = np.where(np.tile(before, (1, HEADS)), 0.0, -np.inf).astype(np.float32)
    diag = np.tile(np.eye(SSD_CHUNK, dtype=np.float32), (1, HEADS))
    e_mat = np.repeat(np.eye(HEADS, dtype=np.float32), HEADDIM, axis=1)
    return (jnp.asarray(e_mat, BF16), jnp.asarray(tri, BF16), jnp.asarray(causal), jnp.asarray(diag))


def _merge_body(x_ref, y5_ref, yssd_ref, z_ref, gl_ref, nw_ref, bg_ref, wglu_ref, bglu_ref,
                wp5_ref, wpssd_ref, wout_ref, o_ref, scr_ref):
    y = _load_transposed(y5_ref, scr_ref)
    gl = 0.5 * y * (1.0 + jnp.tanh(np.float32(np.sqrt(2.0 / np.pi)) * (y + 0.044715 * (y * y * y))))
    glu = gl * _sigmoid(_dot(gl.astype(BF16), wglu_ref[...]) + bglu_ref[...])
    p5 = _dot(glu.astype(BF16), wp5_ref[...])
    t = yssd_ref[...].astype(F32) * _silu(z_ref[...].astype(F32))
    tn = (t * _rms_scale(t) * nw_ref[...]).astype(BF16)
    pssd = _dot(tn, wpssd_ref[...])
    gates = _sigmoid(gl_ref[...].astype(F32) + bg_ref[...])
    merged = gates[:, :D_MODEL] * p5 + gates[:, D_MODEL:] * pssd
    o_ref[...] = x_ref[...] + _dot(merged.astype(BF16), wout_ref[...])


def _merge(x, y5, yssd, z, glogit, nw, bg, wglu, bglu, wp5, wpssd, wout, *, tm=256):
    t = x.shape[0]
    tm = min(tm, t)

    def tile(c):
        return pl.BlockSpec((tm, c), lambda i: (i, 0))

    return pl.pallas_call(
        _merge_body,
        grid=(t // tm,),
        in_specs=[tile(D_MODEL), tile(D_MODEL), tile(D_INNER), tile(D_INNER), tile(2 * D_MODEL)]
        + [_whole()] * 7,
        out_specs=tile(D_MODEL),
        out_shape=jax.ShapeDtypeStruct((t, D_MODEL), F32),
        scratch_shapes=[pltpu.VMEM((D_MODEL // LANES, tm, LANES), F32)],
        compiler_params=_params(("parallel",)),
        name="merge",
    )(x, y5, yssd, z, glogit, nw, bg, wglu, bglu, wp5, wpssd, wout)


def kernel(x, ffn1_norm, ffn1_w_gate, ffn1_w_up, ffn1_w_down, mix_norm, w_in, conv_w, conv_b, s5_A_re, s5_A_im, s5_log_dt, s5_B_re, s5_B_im, s5_C_re, s5_C_im, s5_D, s5_w_glu, s5_b_glu, ssd_A_log, ssd_dt_bias, ssd_D, ssd_norm, w_proj_s5, w_proj_ssd, b_gate, w_out, ffn2_norm, ffn2_w_gate, ffn2_w_up, ffn2_w_down, final_norm):
    bsz, seq, _ = x.shape
    assert SUBLANES % bsz == 0, "the S5 row interleave needs the batch to divide the sublane count"
    assert seq % ROW_BLOCK == 0
    t = bsz * seq
    depth = ffn1_norm.shape[0]
    xt = x.reshape(t, D_MODEL)
    row = lambda v: v.reshape(1, -1).astype(F32)
    fin = row(final_norm)
    e_mat, tri, causal, diag = _ssd_constants()
    dt_end = MAIN_COLS + HEADS

    for i in range(depth):
        xt = _ffn(xt, row(ffn1_norm[i]), ffn1_w_gate[i].astype(BF16), ffn1_w_up[i].astype(BF16),
                  ffn1_w_down[i].astype(BF16), fin, first=(i == 0), final=False)

        w = w_in[i]
        w_dt = jnp.pad(w[:, MAIN_COLS:dt_end], ((0, 0), (0, DT_PAD - HEADS))).astype(BF16)
        dtb = jnp.pad(ssd_dt_bias[i], (0, DT_PAD - HEADS)).reshape(1, DT_PAD)
        u, z, xbc, dt, glogit = _inproj(
            xt, row(mix_norm[i]), w[:, :MAIN_COLS].astype(BF16), w_dt, w[:, dt_end:].astype(BF16),
            conv_w[i], row(conv_b[i]), dtb, batch=bsz)

        w1, cbd, apow = _s5_operators(s5_A_re[i], s5_A_im[i], s5_log_dt[i], s5_B_re[i], s5_B_im[i],
                                      s5_C_re[i], s5_C_im[i], SUBLANES // bsz)
        y5 = _s5(u.reshape(bsz, seq, D_MODEL), w1, cbd, apow, row(s5_D[i])).reshape(t, D_MODEL)

        yssd = _ssd(xbc, dt, row(ssd_A_log[i]), row(jnp.repeat(ssd_D[i], HEADDIM)), e_mat, tri, causal, diag,
                    batch=bsz)

        xt = _merge(xt, y5, yssd, z, glogit, row(ssd_norm[i]), row(b_gate[i]), s5_w_glu[i].astype(BF16),
                    row(s5_b_glu[i]), w_proj_s5[i].astype(BF16), w_proj_ssd[i].astype(BF16),
                    w_out[i].astype(BF16))

        xt = _ffn(xt, row(ffn2_norm[i]), ffn2_w_gate[i].astype(BF16), ffn2_w_up[i].astype(BF16),
                  ffn2_w_down[i].astype(BF16), fin, first=False, final=(i == depth - 1))
    return xt.reshape(bsz, seq, D_MODEL)
```

```python
import functools

import jax
import jax.numpy as jnp
import numpy as np
from jax import lax
from jax.experimental import pallas as pl
from jax.experimental.pallas import tpu as pltpu

EPS = 1e-6
D_MODEL = 1024
D_FF = 2816
S5_GROUP = 16
S5_GROUPS = 64
S5_STATE = 64
D_INNER = 2048
HEADDIM = 64
HEADS = 32
SSD_GROUPS = 8
HPG = 4
SSD_STATE = 128
CONV_K = 4
CONV_DIM = 4096
BC_WIDTH = SSD_GROUPS * SSD_STATE
GROUP_COLS = HPG * HEADDIM

LANES = 128
SUBLANES = 8
ROW_BLOCK = SUBLANES * SUBLANES
S5_BLOCK_GROUPS = LANES // S5_GROUP
S5_BLOCKS = S5_GROUPS // S5_BLOCK_GROUPS
S5_BLOCK_STATE = S5_BLOCK_GROUPS * S5_STATE
S5_PAIR = 2
S5_SUB_ROWS = 512
SSD_CHUNK = ROW_BLOCK
DT_PAD = LANES
PROJ_COLS = 512
FF_COLS = 256
MAIN_COLS = D_MODEL + D_INNER + CONV_DIM

VMEM_LIMIT = 56 * 1024 * 1024

F32 = jnp.float32
BF16 = jnp.bfloat16


def _dot(a, b):
    return jnp.dot(a, b, preferred_element_type=F32)


def _sigmoid(v):
    return 1.0 / (1.0 + jnp.exp(-v))


def _silu(v):
    return v * _sigmoid(v)


def _rms_scale(xf):
    return lax.rsqrt(jnp.mean(xf * xf, axis=-1, keepdims=True) + EPS)


def _split3(v):
    hi = v.astype(BF16)
    r1 = v - hi.astype(F32)
    mid = r1.astype(BF16)
    lo = (r1 - mid.astype(F32)).astype(BF16)
    return hi, mid, lo


def _dot_exact_rhs(v, m_bf16):
    hi, mid, lo = _split3(v)
    return _dot(hi, m_bf16) + _dot(mid, m_bf16) + _dot(lo, m_bf16)


def _block_transpose(v):
    rows, cols = v.shape
    return jnp.swapaxes(v.reshape(rows // ROW_BLOCK, SUBLANES, SUBLANES, cols), 1, 2).reshape(rows, cols)


def _whole():
    return pl.BlockSpec(memory_space=pltpu.VMEM)


def _params(sem):
    return pltpu.CompilerParams(dimension_semantics=sem, vmem_limit_bytes=VMEM_LIMIT)


def _ffn_body(x_ref, g_ref, wg_ref, wu_ref, wd_ref, fin_ref, o_ref, *, first, final):
    x = _block_transpose(x_ref[...]) if first else x_ref[...]
    h = (x * _rms_scale(x) * g_ref[...]).astype(BF16)
    acc = jnp.zeros(x.shape, F32)
    for c0 in range(0, D_FF, FF_COLS):
        sl = slice(c0, c0 + FF_COLS)
        a = _dot(h, wg_ref[:, sl])
        b = _dot(h, wu_ref[:, sl])
        act = (_silu(a) * b).astype(BF16)
        acc = acc + _dot(act, wd_ref[sl, :])
    y = x + 0.5 * acc
    if final:
        y = y * _rms_scale(y) * fin_ref[...]
        o_ref[...] = _block_transpose(y)
    else:
        o_ref[...] = y


def _ffn(x, g, wg, wu, wd, fin, *, first, final, tm=512):
    t = x.shape[0]
    tm = min(tm, t)
    tile = pl.BlockSpec((tm, D_MODEL), lambda i: (i, 0))
    return pl.pallas_call(
        functools.partial(_ffn_body, first=first, final=final),
        grid=(t // tm,),
        in_specs=[tile, _whole(), _whole(), _whole(), _whole(), _whole()],
        out_specs=tile,
        out_shape=jax.ShapeDtypeStruct((t, D_MODEL), F32),
        compiler_params=_params(("parallel",)),
        name="ffn_final" if final else "ffn",
    )(x, g, wg, wu, wd, fin)


def _inproj_body(x_ref, g_ref, wmain_ref, wdt_ref, wgt_ref, cw_ref, cb_ref, dtb_ref,
                 u_ref, z_ref, xbc_ref, dt_ref, gl_ref, tail_ref):
    tm = x_ref.shape[0]
    nblk = tm // ROW_BLOCK
    x = x_ref[...]
    h = (x * _rms_scale(x) * g_ref[...]).astype(BF16)
    for c0 in range(0, D_MODEL, PROJ_COLS):
        u_ref[:, c0:c0 + PROJ_COLS] = _block_transpose(_dot(h, wmain_ref[:, c0:c0 + PROJ_COLS])).astype(BF16)
    for c0 in range(0, D_INNER, PROJ_COLS):
        z_ref[:, c0:c0 + PROJ_COLS] = _dot(h, wmain_ref[:, D_MODEL + c0:D_MODEL + c0 + PROJ_COLS]).astype(BF16)
    for c0 in range(0, 2 * D_MODEL, PROJ_COLS):
        gl_ref[:, c0:c0 + PROJ_COLS] = _dot(h, wgt_ref[:, c0:c0 + PROJ_COLS]).astype(BF16)
    raw = _dot(h, wdt_ref[...]) + dtb_ref[...]
    dt_ref[...] = jnp.maximum(raw, 0.0) + jnp.log1p(jnp.exp(-jnp.abs(raw)))

    @pl.when(pl.program_id(1) == 0)
    def _():
        tail_ref[...] = jnp.zeros(tail_ref.shape, F32)

    nd = CONV_K - 1
    sub = lax.broadcasted_iota(jnp.int32, (nblk, nd, SUBLANES, PROJ_COLS), 2)
    base = D_MODEL + D_INNER
    for c0 in range(0, CONV_DIM, PROJ_COLS):
        cs = slice(c0, c0 + PROJ_COLS)
        v = _dot(h, wmain_ref[:, base + c0:base + c0 + PROJ_COLS]).reshape(nblk, SUBLANES, SUBLANES, PROJ_COLS)
        rolled = pltpu.roll(v[:, SUBLANES - nd:].reshape(nblk * nd, SUBLANES, PROJ_COLS), 1, axis=1)
        rolled = rolled.reshape(nblk, nd, SUBLANES, PROJ_COLS)
        carry = tail_ref[:, cs].reshape(1, nd, SUBLANES, PROJ_COLS)
        tail_ref[:, cs] = rolled[nblk - 1].reshape(nd * SUBLANES, PROJ_COLS)
        prev = jnp.concatenate([carry, rolled[:nblk - 1]], axis=0) if nblk > 1 else carry
        edge = jnp.where(sub >= 1, rolled, prev)
        w_tap = lambda k: cw_ref[k:k + 1, cs].reshape(1, 1, 1, PROJ_COLS)
        acc = cb_ref[:, cs].reshape(1, 1, 1, PROJ_COLS) + w_tap(CONV_K - 1) * v
        for j in range(1, CONV_K):
            delayed = jnp.concatenate([edge[:, nd - j:], v[:, :SUBLANES - j]], axis=1)
            acc = acc + w_tap(CONV_K - 1 - j) * delayed
        xbc_ref[:, cs] = _silu(acc).reshape(tm, PROJ_COLS).astype(BF16)


def _inproj(x, g, wmain, wdt, wgt, cw, cb, dtb, *, batch, tm=256):
    t = x.shape[0]
    seq = t // batch
    tm = min(tm, seq)
    nt = seq // tm

    def tile(c):
        return pl.BlockSpec((tm, c), lambda b, i: (b * nt + i, 0))

    outs = ((D_MODEL, BF16), (D_INNER, BF16), (CONV_DIM, BF16), (DT_PAD, F32), (2 * D_MODEL, BF16))
    return pl.pallas_call(
        _inproj_body,
        grid=(batch, nt),
        in_specs=[tile(D_MODEL)] + [_whole()] * 7,
        out_specs=[tile(c) for c, _ in outs],
        out_shape=[jax.ShapeDtypeStruct((t, c), d) for c, d in outs],
        scratch_shapes=[pltpu.VMEM(((CONV_K - 1) * SUBLANES, CONV_DIM), F32)],
        compiler_params=_params(("parallel", "arbitrary")),
        name="inproj",
    )(x, g, wmain, wdt, wgt, cw, cb, dtb)


def _s5_body(u_ref, w1_ref, cbd_ref, apow_ref, dskip_ref, y_ref, ubuf_ref, bp_ref, xs_ref, x_ref, ybuf_ref,
             *, n_sub):
    bsz, tt, _ = u_ref.shape
    m = bsz * tt
    ns = S5_PAIR * S5_BLOCK_STATE
    i = pl.program_id(1)

    @pl.when(i == 0)
    def _():
        ubuf_ref[:, 0:SUBLANES, :] = jnp.zeros((S5_PAIR, SUBLANES, LANES), F32)
        x_ref[...] = jnp.zeros(x_ref.shape, F32)

    @pl.when(i > 0)
    def _():
        ubuf_ref[:, 0:SUBLANES, :] = ubuf_ref[:, m:m + SUBLANES, :]

    for q in range(S5_PAIR):
        for b in range(bsz):
            ubuf_ref[q, pl.ds(SUBLANES + b, tt, stride=bsz), :] = u_ref[b, :, q * LANES:(q + 1) * LANES].astype(F32)

    sub_rows = min(S5_SUB_ROWS, m)
    for r0 in range(0, m, sub_rows):
        for q in range(S5_PAIR):
            ucat = jnp.concatenate(
                [ubuf_ref[q, SUBLANES - k * bsz + r0:SUBLANES - k * bsz + r0 + sub_rows, :].astype(BF16)
                 for k in range(n_sub)], axis=1)
            bp_ref[r0:r0 + sub_rows, q * S5_BLOCK_STATE:(q + 1) * S5_BLOCK_STATE] = _dot(ucat, w1_ref[q, 0])
            bp_ref[r0:r0 + sub_rows, ns + q * S5_BLOCK_STATE:ns + (q + 1) * S5_BLOCK_STATE] = _dot(ucat, w1_ref[q, 1])

    a_r = jnp.broadcast_to(apow_ref[0, 0:1, :], (SUBLANES, ns))
    a_i = jnp.broadcast_to(apow_ref[0, 1:2, :], (SUBLANES, ns))
    pair = 2 * SUBLANES
    xr, xi = x_ref[:, 0:ns], x_ref[:, ns:2 * ns]
    for r in range(0, m, pair):
        re0 = a_r * xr - a_i * xi + bp_ref[r:r + SUBLANES, 0:ns]
        im0 = a_r * xi + a_i * xr + bp_ref[r:r + SUBLANES, ns:2 * ns]
        xr = a_r * re0 - a_i * im0 + bp_ref[r + SUBLANES:r + pair, 0:ns]
        xi = a_r * im0 + a_i * re0 + bp_ref[r + SUBLANES:r + pair, ns:2 * ns]
        xs_ref[r:r + pair, 0:ns] = jnp.concatenate([re0, xr], axis=0).astype(BF16)
        xs_ref[r:r + pair, ns:2 * ns] = jnp.concatenate([im0, xi], axis=0).astype(BF16)
    x_ref[:, 0:ns] = xr
    x_ref[:, ns:2 * ns] = xi

    for r0 in range(0, m, sub_rows):
        y = _dot(xs_ref[r0:r0 + sub_rows, :], cbd_ref[0])
        for q in range(S5_PAIR):
            cols = slice(q * LANES, (q + 1) * LANES)
            ybuf_ref[q, r0:r0 + sub_rows, :] = (
                y[:, cols] + dskip_ref[:, cols] * ubuf_ref[q, SUBLANES + r0:SUBLANES + r0 + sub_rows, :])
    for q in range(S5_PAIR):
        for b in range(bsz):
            y_ref[b, :, q * LANES:(q + 1) * LANES] = ybuf_ref[q, pl.ds(b, tt, stride=bsz), :]


def _s5(u3, w1, cbd, apow, dskip, *, tt=512):
    bsz, seq, _ = u3.shape
    n_sub = SUBLANES // bsz
    tt = min(tt, seq)
    m = bsz * tt
    width = S5_PAIR * LANES
    ns = S5_PAIR * S5_BLOCK_STATE
    blk = pl.BlockSpec((bsz, tt, width), lambda j, i: (0, i, j))
    return pl.pallas_call(
        functools.partial(_s5_body, n_sub=n_sub),
        grid=(S5_BLOCKS // S5_PAIR, seq // tt),
        in_specs=[blk,
                  pl.BlockSpec((S5_PAIR, 2, n_sub * LANES, S5_BLOCK_STATE), lambda j, i: (j, 0, 0, 0)),
                  pl.BlockSpec((1, 2 * ns, width), lambda j, i: (j, 0, 0)),
                  pl.BlockSpec((1, 2, ns), lambda j, i: (j, 0, 0)),
                  pl.BlockSpec((1, width), lambda j, i: (0, j))],
        out_specs=blk,
        out_shape=jax.ShapeDtypeStruct((bsz, seq, D_MODEL), F32),
        scratch_shapes=[pltpu.VMEM((S5_PAIR, m + SUBLANES, LANES), F32),
                        pltpu.VMEM((m, 2 * ns), F32),
                        pltpu.VMEM((m, 2 * ns), BF16),
                        pltpu.VMEM((SUBLANES, 2 * ns), F32),
                        pltpu.VMEM((S5_PAIR, m, LANES), F32)],
        compiler_params=_params(("parallel", "arbitrary")),
        name="s5",
    )(u3, w1, cbd, apow, dskip)


def _s5_operators(a_re, a_im, log_dt, b_re, b_im, c_re, c_im, n_sub):
    dt = jnp.exp(log_dt)[:, None]
    lr, li = a_re, a_im
    mag = jnp.exp(lr * dt)
    ar, ai = mag * jnp.cos(li * dt), mag * jnp.sin(li * dt)
    den = lr * lr + li * li
    cr = ((ar - 1.0) * lr + ai * li) / den
    ci = (ai * lr - (ar - 1.0) * li) / den
    bbr = cr[..., None] * b_re - ci[..., None] * b_im
    bbi = cr[..., None] * b_im + ci[..., None] * b_re

    def power(k):
        m = jnp.exp(k * lr * dt)
        return m * jnp.cos(k * li * dt), m * jnp.sin(k * li * dt)

    eye = jnp.eye(S5_BLOCK_GROUPS, dtype=F32)

    def blockdiag_in(w):
        w = w.reshape(S5_BLOCKS, S5_BLOCK_GROUPS, S5_STATE, S5_GROUP)
        return jnp.einsum('jgnm,gh->jgmhn', w, eye).reshape(S5_BLOCKS, LANES, S5_BLOCK_STATE)

    def blockdiag_out(c):
        c = c.reshape(S5_BLOCKS, S5_BLOCK_GROUPS, S5_GROUP, S5_STATE)
        return jnp.einsum('jgmn,gh->jgnhm', c, eye).reshape(S5_BLOCKS, S5_BLOCK_STATE, LANES)

    rows = []
    for k in range(n_sub):
        pr, pi = power(float(k))
        wr = pr[..., None] * bbr - pi[..., None] * bbi
        wi = pr[..., None] * bbi + pi[..., None] * bbr
        rows.append(jnp.stack([blockdiag_in(wr), blockdiag_in(wi)], axis=1))
    w1 = jnp.concatenate(rows, axis=2).astype(BF16)
    eye_p = jnp.eye(S5_PAIR, dtype=F32)

    def pair_out(c):
        c = c.reshape(S5_BLOCKS // S5_PAIR, S5_PAIR, S5_BLOCK_STATE, LANES)
        return jnp.einsum('jqnm,qr->jqnrm', c, eye_p).reshape(S5_BLOCKS // S5_PAIR, S5_PAIR * S5_BLOCK_STATE,
                                                              S5_PAIR * LANES)

    cbd = jnp.concatenate([pair_out(blockdiag_out(c_re)), pair_out(blockdiag_out(-c_im))], axis=1).astype(BF16)
    sr, si = power(float(n_sub))
    apow = jnp.stack([sr.reshape(S5_BLOCKS // S5_PAIR, S5_PAIR * S5_BLOCK_STATE),
                      si.reshape(S5_BLOCKS // S5_PAIR, S5_PAIR * S5_BLOCK_STATE)], axis=1)
    return w1, cbd, apow


def _ssd_chunk(xbc, dt_full, alog_ref, dexp_ref, e_ref, tri_ref, causal_ref, diag_ref, h_ref):
    lc = SSD_CHUNK
    xs_b = xbc[:, :D_INNER]
    xs = xs_b.astype(F32)
    bm = xbc[:, D_INNER:D_INNER + BC_WIDTH]
    cm = xbc[:, D_INNER + BC_WIDTH:]

    dt = dt_full[:, :HEADS]
    da = dt * (-jnp.exp(alog_ref[...]))
    hi, mid, lo = _split3(da)
    tri = tri_ref[...]
    a_cum = _dot(tri, hi) + _dot(tri, mid) + _dot(tri, lo)

    e = e_ref[...]
    a_e = _dot_exact_rhs(a_cum, e)
    dt_e = _dot_exact_rhs(dt, e)
    diag = diag_ref[...]
    a_j = jnp.sum(a_e * diag, axis=0, keepdims=True)
    dt_j = jnp.sum(dt_e * diag, axis=0, keepdims=True)
    lmat = jnp.exp(a_e - a_j + causal_ref[...])
    a_last = a_e[lc - 1:lc, :]
    exp_a = jnp.exp(a_e)
    xw = (xs * (jnp.exp(a_last - a_e) * dt_e)).astype(BF16)
    chunk_decay = jnp.exp(a_last)

    blk_r = lax.broadcasted_iota(jnp.int32, (GROUP_COLS, GROUP_COLS), 0) // HEADDIM
    blk_c = lax.broadcasted_iota(jnp.int32, (GROUP_COLS, GROUP_COLS), 1) // HEADDIM
    same_head = blk_r == blk_c
    nt = (((1,), (1,)), ((), ()))
    tn = (((0,), (0,)), ((), ()))
    ys = []
    for g in range(SSD_GROUPS):
        cs = slice(g * GROUP_COLS, (g + 1) * GROUP_COLS)
        ns = slice(g * SSD_STATE, (g + 1) * SSD_STATE)
        bm_g, cm_g = bm[:, ns], cm[:, ns]
        cb = lax.dot_general(cm_g, jnp.concatenate([bm_g] * HPG, axis=0), nt,
                             preferred_element_type=F32)
        w = (cb * lmat[:, cs] * dt_j[:, cs]).astype(BF16)
        x_g = xs_b[:, cs]
        x_bd = jnp.where(same_head, jnp.concatenate([x_g] * HPG, axis=0), jnp.zeros((), BF16))
        y_diag = _dot(w, x_bd)
        h_prev = h_ref[g]
        y_off = _dot(cm_g, h_prev.astype(BF16)) * exp_a[:, cs]
        ys.append(y_diag + y_off)
        st = lax.dot_general(bm_g, xw[:, cs], tn, preferred_element_type=F32)
        h_ref[g] = h_prev * chunk_decay[:, cs] + st

    return (jnp.concatenate(ys, axis=1) + dexp_ref[...] * xs).astype(BF16)


def _ssd_body(xbc_ref, dt_ref, alog_ref, dexp_ref, e_ref, tri_ref, causal_ref, diag_ref, y_ref, h_ref, *, cps):
    lc = SSD_CHUNK

    @pl.when(pl.program_id(1) == 0)
    def _():
        h_ref[...] = jnp.zeros(h_ref.shape, F32)

    for ci in range(cps):
        rs = slice(ci * lc, (ci + 1) * lc)
        y_ref[rs, :] = _ssd_chunk(xbc_ref[rs, :], dt_ref[rs, :], alog_ref, dexp_ref, e_ref, tri_ref,
                                  causal_ref, diag_ref, h_ref)


def _ssd(xbc, dt, alog, dexp, e, tri, causal, diag, *, batch, cps=2):
    t = xbc.shape[0]
    seq = t // batch
    cps = min(cps, seq // SSD_CHUNK)
    rows = cps * SSD_CHUNK
    ns = seq // rows

    def tile(c):
        return pl.BlockSpec((rows, c), lambda b, i: (b * ns + i, 0))

    return pl.pallas_call(
        functools.partial(_ssd_body, cps=cps),
        grid=(batch, ns),
        in_specs=[tile(CONV_DIM), tile(DT_PAD)] + [_whole()] * 6,
        out_specs=tile(D_INNER),
        out_shape=jax.ShapeDtypeStruct((t, D_INNER), BF16),
        scratch_shapes=[pltpu.VMEM((SSD_GROUPS, SSD_STATE, GROUP_COLS), F32)],
        compiler_params=_params(("parallel", "arbitrary")),
        name="ssd",
    )(xbc, dt, alog, dexp, e, tri, causal, diag)


def _ssd_constants():
    r = np.arange(SSD_CHUNK)
    time = SUBLANES * (r % SUBLANES) + r // SUBLANES
    before = time[None, :] <= time[:, None]
    tri = before.astype(np.float32)
    causal = np.where(np.tile(before, (1, HEADS)), 0.0, -np.inf).astype(np.float32)
    diag = np.tile(np.eye(SSD_CHUNK, dtype=np.float32), (1, HEADS))
    e_mat = np.repeat(np.eye(HEADS, dtype=np.float32), HEADDIM, axis=1)
    return (jnp.asarray(e_mat, BF16), jnp.asarray(tri, BF16), jnp.asarray(causal), jnp.asarray(diag))


def _merge_body(x_ref, y5_ref, yssd_ref, z_ref, gl_ref, nw_ref, bg_ref, wglu_ref, bglu_ref,
                wp5_ref, wpssd_ref, wout_ref, o_ref):
    y = _block_transpose(y5_ref[...])
    gl = 0.5 * y * (1.0 + jnp.tanh(np.float32(np.sqrt(2.0 / np.pi)) * (y + 0.044715 * (y * y * y))))
    glu = gl * _sigmoid(_dot(gl.astype(BF16), wglu_ref[...]) + bglu_ref[...])
    p5 = _dot(glu.astype(BF16), wp5_ref[...])
    t = yssd_ref[...].astype(F32) * _silu(z_ref[...].astype(F32))
    tn = (t * _rms_scale(t) * nw_ref[...]).astype(BF16)
    pssd = _dot(tn, wpssd_ref[...])
    gates = _sigmoid(gl_ref[...].astype(F32) + bg_ref[...])
    merged = gates[:, :D_MODEL] * p5 + gates[:, D_MODEL:] * pssd
    o_ref[...] = x_ref[...] + _dot(merged.astype(BF16), wout_ref[...])


def _merge(x, y5, yssd, z, glogit, nw, bg, wglu, bglu, wp5, wpssd, wout, *, tm=256):
    t = x.shape[0]
    tm = min(tm, t)

    def tile(c):
        return pl.BlockSpec((tm, c), lambda i: (i, 0))

    return pl.pallas_call(
        _merge_body,
        grid=(t // tm,),
        in_specs=[tile(D_MODEL), tile(D_MODEL), tile(D_INNER), tile(D_INNER), tile(2 * D_MODEL)]
        + [_whole()] * 7,
        out_specs=tile(D_MODEL),
        out_shape=jax.ShapeDtypeStruct((t, D_MODEL), F32),
        compiler_params=_params(("parallel",)),
        name="merge",
    )(x, y5, yssd, z, glogit, nw, bg, wglu, bglu, wp5, wpssd, wout)


def kernel(x, ffn1_norm, ffn1_w_gate, ffn1_w_up, ffn1_w_down, mix_norm, w_in, conv_w, conv_b, s5_A_re, s5_A_im, s5_log_dt, s5_B_re, s5_B_im, s5_C_re, s5_C_im, s5_D, s5_w_glu, s5_b_glu, ssd_A_log, ssd_dt_bias, ssd_D, ssd_norm, w_proj_s5, w_proj_ssd, b_gate, w_out, ffn2_norm, ffn2_w_gate, ffn2_w_up, ffn2_w_down, final_norm):
    bsz, seq, _ = x.shape
    assert SUBLANES % bsz == 0, "the S5 row interleave needs the batch to divide the sublane count"
    assert seq % ROW_BLOCK == 0
    t = bsz * seq
    depth = ffn1_norm.shape[0]
    xt = x.reshape(t, D_MODEL)
    row = lambda v: v.reshape(1, -1).astype(F32)
    fin = row(final_norm)
    e_mat, tri, causal, diag = _ssd_constants()
    dt_end = MAIN_COLS + HEADS

    for i in range(depth):
        xt = _ffn(xt, row(ffn1_norm[i]), ffn1_w_gate[i].astype(BF16), ffn1_w_up[i].astype(BF16),
                  ffn1_w_down[i].astype(BF16), fin, first=(i == 0), final=False)

        w = w_in[i]
        w_dt = jnp.pad(w[:, MAIN_COLS:dt_end], ((0, 0), (0, DT_PAD - HEADS))).astype(BF16)
        dtb = jnp.pad(ssd_dt_bias[i], (0, DT_PAD - HEADS)).reshape(1, DT_PAD)
        u, z, xbc, dt, glogit = _inproj(
            xt, row(mix_norm[i]), w[:, :MAIN_COLS].astype(BF16), w_dt, w[:, dt_end:].astype(BF16),
            conv_w[i], row(conv_b[i]), dtb, batch=bsz)

        w1, cbd, apow = _s5_operators(s5_A_re[i], s5_A_im[i], s5_log_dt[i], s5_B_re[i], s5_B_im[i],
                                      s5_C_re[i], s5_C_im[i], SUBLANES // bsz)
        y5 = _s5(u.reshape(bsz, seq, D_MODEL), w1, cbd, apow, row(s5_D[i])).reshape(t, D_MODEL)

        yssd = _ssd(xbc, dt, row(ssd_A_log[i]), row(jnp.repeat(ssd_D[i], HEADDIM)), e_mat, tri, causal, diag,
                    batch=bsz)

        xt = _merge(xt, y5, yssd, z, glogit, row(ssd_norm[i]), row(b_gate[i]), s5_w_glu[i].astype(BF16),
                    row(s5_b_glu[i]), w_proj_s5[i].astype(BF16), w_proj_ssd[i].astype(BF16),
                    w_out[i].astype(BF16))

        xt = _ffn(xt, row(ffn2_norm[i]), ffn2_w_gate[i].astype(BF16), ffn2_w_up[i].astype(BF16),
                  ffn2_w_down[i].astype(BF16), fin, first=False, final=(i == depth - 1))
    return xt.reshape(bsz, seq, D_MODEL)
```

```python
import functools

import jax
import jax.numpy as jnp
import numpy as np
from jax import lax
from jax.experimental import pallas as pl
from jax.experimental.pallas import tpu as pltpu

EPS = 1e-6
D_MODEL = 1024
D_FF = 2816
S5_GROUP = 16
S5_GROUPS = 64
S5_STATE = 64
D_INNER = 2048
HEADDIM = 64
HEADS = 32
SSD_GROUPS = 8
HPG = 4
SSD_STATE = 128
CONV_K = 4
CONV_DIM = 4096
BC_WIDTH = SSD_GROUPS * SSD_STATE
GROUP_COLS = HPG * HEADDIM

LANES = 128
SUBLANES = 8
ROW_BLOCK = SUBLANES * SUBLANES
S5_BLOCK_GROUPS = LANES // S5_GROUP
S5_BLOCKS = S5_GROUPS // S5_BLOCK_GROUPS
S5_BLOCK_STATE = S5_BLOCK_GROUPS * S5_STATE
S5_PAIR = 2
S5_SUB_ROWS = 512
SSD_CHUNK = ROW_BLOCK
DT_PAD = LANES
PROJ_COLS = 512
FF_COLS = 256
MAIN_COLS = D_MODEL + D_INNER + CONV_DIM

VMEM_LIMIT = 56 * 1024 * 1024

F32 = jnp.float32
BF16 = jnp.bfloat16


def _dot(a, b):
    return jnp.dot(a, b, preferred_element_type=F32)


def _sigmoid(v):
    return 1.0 / (1.0 + jnp.exp(-v))


def _silu(v):
    return v * _sigmoid(v)


def _rms_scale(xf):
    return lax.rsqrt(jnp.mean(xf * xf, axis=-1, keepdims=True) + EPS)


def _split3(v):
    hi = v.astype(BF16)
    r1 = v - hi.astype(F32)
    mid = r1.astype(BF16)
    lo = (r1 - mid.astype(F32)).astype(BF16)
    return hi, mid, lo


def _block_transpose(v):
    rows, cols = v.shape
    return jnp.swapaxes(v.reshape(rows // ROW_BLOCK, SUBLANES, SUBLANES, cols), 1, 2).reshape(rows, cols)


def _whole():
    return pl.BlockSpec(memory_space=pltpu.VMEM)


def _params(sem):
    return pltpu.CompilerParams(dimension_semantics=sem, vmem_limit_bytes=VMEM_LIMIT)


def _ffn_body(x_ref, g_ref, wg_ref, wu_ref, wd_ref, fin_ref, o_ref, *, first, final):
    x = _block_transpose(x_ref[...]) if first else x_ref[...]
    h = (x * _rms_scale(x) * g_ref[...]).astype(BF16)
    acc = jnp.zeros(x.shape, F32)
    for c0 in range(0, D_FF, FF_COLS):
        sl = slice(c0, c0 + FF_COLS)
        a = _dot(h, wg_ref[:, sl])
        b = _dot(h, wu_ref[:, sl])
        act = (_silu(a) * b).astype(BF16)
        acc = acc + _dot(act, wd_ref[sl, :])
    y = x + 0.5 * acc
    if final:
        y = y * _rms_scale(y) * fin_ref[...]
        o_ref[...] = _block_transpose(y)
    else:
        o_ref[...] = y


def _ffn(x, g, wg, wu, wd, fin, *, first, final, tm=512):
    t = x.shape[0]
    tm = min(tm, t)
    tile = pl.BlockSpec((tm, D_MODEL), lambda i: (i, 0))
    return pl.pallas_call(
        functools.partial(_ffn_body, first=first, final=final),
        grid=(t // tm,),
        in_specs=[tile, _whole(), _whole(), _whole(), _whole(), _whole()],
        out_specs=tile,
        out_shape=jax.ShapeDtypeStruct((t, D_MODEL), F32),
        compiler_params=_params(("parallel",)),
        name="ffn_final" if final else "ffn",
    )(x, g, wg, wu, wd, fin)


def _inproj_body(x_ref, g_ref, wu_ref, wx_ref, wdt_ref, cw_ref, cb_ref, dtb_ref,
                 u_ref, xbc_ref, dt_ref, tail_ref):
    tm = x_ref.shape[0]
    nblk = tm // ROW_BLOCK
    x = x_ref[...]
    h = (x * _rms_scale(x) * g_ref[...]).astype(BF16)
    for c0 in range(0, D_MODEL, PROJ_COLS):
        u_ref[:, c0:c0 + PROJ_COLS] = _block_transpose(_dot(h, wu_ref[:, c0:c0 + PROJ_COLS])).astype(BF16)
    raw = _dot(h, wdt_ref[...]) + dtb_ref[...]
    dt_ref[...] = jnp.maximum(raw, 0.0) + jnp.log1p(jnp.exp(-jnp.abs(raw)))

    @pl.when(pl.program_id(1) == 0)
    def _():
        tail_ref[...] = jnp.zeros(tail_ref.shape, F32)

    nd = CONV_K - 1
    sub = lax.broadcasted_iota(jnp.int32, (nblk, nd, SUBLANES, PROJ_COLS), 2)
    for c0 in range(0, CONV_DIM, PROJ_COLS):
        cs = slice(c0, c0 + PROJ_COLS)
        v = _dot(h, wx_ref[:, cs]).reshape(nblk, SUBLANES, SUBLANES, PROJ_COLS)
        rolled = pltpu.roll(v[:, SUBLANES - nd:].reshape(nblk * nd, SUBLANES, PROJ_COLS), 1, axis=1)
        rolled = rolled.reshape(nblk, nd, SUBLANES, PROJ_COLS)
        carry = tail_ref[:, cs].reshape(1, nd, SUBLANES, PROJ_COLS)
        tail_ref[:, cs] = rolled[nblk - 1].reshape(nd * SUBLANES, PROJ_COLS)
        prev = jnp.concatenate([carry, rolled[:nblk - 1]], axis=0) if nblk > 1 else carry
        edge = jnp.where(sub >= 1, rolled, prev)
        w_tap = lambda k: cw_ref[k:k + 1, cs].reshape(1, 1, 1, PROJ_COLS)
        acc = cb_ref[:, cs].reshape(1, 1, 1, PROJ_COLS) + w_tap(CONV_K - 1) * v
        for j in range(1, CONV_K):
            delayed = jnp.concatenate([edge[:, nd - j:], v[:, :SUBLANES - j]], axis=1)
            acc = acc + w_tap(CONV_K - 1 - j) * delayed
        xbc_ref[:, cs] = _silu(acc).reshape(tm, PROJ_COLS).astype(BF16)


def _inproj(x, g, wu, wx, wdt, cw, cb, dtb, *, batch, tm=512):
    t = x.shape[0]
    seq = t // batch
    tm = min(tm, seq)
    nt = seq // tm

    def tile(c):
        return pl.BlockSpec((tm, c), lambda b, i: (b * nt + i, 0))

    outs = ((D_MODEL, BF16), (CONV_DIM, BF16), (DT_PAD, F32))
    return pl.pallas_call(
        _inproj_body,
        grid=(batch, nt),
        in_specs=[tile(D_MODEL)] + [_whole()] * 7,
        out_specs=[tile(c) for c, _ in outs],
        out_shape=[jax.ShapeDtypeStruct((t, c), d) for c, d in outs],
        scratch_shapes=[pltpu.VMEM(((CONV_K - 1) * SUBLANES, CONV_DIM), F32)],
        compiler_params=_params(("parallel", "arbitrary")),
        name="inproj",
    )(x, g, wu, wx, wdt, cw, cb, dtb)


def _s5_body(u_ref, w1_ref, cbd_ref, apow_ref, dskip_ref, y_ref, ubuf_ref, bp_ref, xs_ref, x_ref, ybuf_ref,
             *, n_sub):
    bsz, tt, _ = u_ref.shape
    m = bsz * tt
    ns = S5_PAIR * S5_BLOCK_STATE
    i = pl.program_id(1)

    @pl.when(i == 0)
    def _():
        ubuf_ref[:, 0:SUBLANES, :] = jnp.zeros((S5_PAIR, SUBLANES, LANES), F32)
        x_ref[...] = jnp.zeros(x_ref.shape, F32)

    @pl.when(i > 0)
    def _():
        ubuf_ref[:, 0:SUBLANES, :] = ubuf_ref[:, m:m + SUBLANES, :]

    for q in range(S5_PAIR):
        for b in range(bsz):
            ubuf_ref[q, pl.ds(SUBLANES + b, tt, stride=bsz), :] = u_ref[b, :, q * LANES:(q + 1) * LANES].astype(F32)

    sub_rows = min(S5_SUB_ROWS, m)
    for r0 in range(0, m, sub_rows):
        for q in range(S5_PAIR):
            ucat = jnp.concatenate(
                [ubuf_ref[q, SUBLANES - k * bsz + r0:SUBLANES - k * bsz + r0 + sub_rows, :].astype(BF16)
                 for k in range(n_sub)], axis=1)
            bp_ref[r0:r0 + sub_rows, q * S5_BLOCK_STATE:(q + 1) * S5_BLOCK_STATE] = _dot(ucat, w1_ref[q, 0])
            bp_ref[r0:r0 + sub_rows, ns + q * S5_BLOCK_STATE:ns + (q + 1) * S5_BLOCK_STATE] = _dot(ucat, w1_ref[q, 1])

    a_r = jnp.broadcast_to(apow_ref[0, 0:1, :], (SUBLANES, ns))
    a_i = jnp.broadcast_to(apow_ref[0, 1:2, :], (SUBLANES, ns))
    pair = 2 * SUBLANES
    xr, xi = x_ref[:, 0:ns], x_ref[:, ns:2 * ns]
    for r in range(0, m, pair):
        re0 = a_r * xr - a_i * xi + bp_ref[r:r + SUBLANES, 0:ns]
        im0 = a_r * xi + a_i * xr + bp_ref[r:r + SUBLANES, ns:2 * ns]
        xr = a_r * re0 - a_i * im0 + bp_ref[r + SUBLANES:r + pair, 0:ns]
        xi = a_r * im0 + a_i * re0 + bp_ref[r + SUBLANES:r + pair, ns:2 * ns]
        xs_ref[r:r + pair, 0:ns] = jnp.concatenate([re0, xr], axis=0).astype(BF16)
        xs_ref[r:r + pair, ns:2 * ns] = jnp.concatenate([im0, xi], axis=0).astype(BF16)
    x_ref[:, 0:ns] = xr
    x_ref[:, ns:2 * ns] = xi

    for r0 in range(0, m, sub_rows):
        y = _dot(xs_ref[r0:r0 + sub_rows, :], cbd_ref[0])
        for q in range(S5_PAIR):
            cols = slice(q * LANES, (q + 1) * LANES)
            ybuf_ref[q, r0:r0 + sub_rows, :] = (
                y[:, cols] + dskip_ref[:, cols] * ubuf_ref[q, SUBLANES + r0:SUBLANES + r0 + sub_rows, :])
    for q in range(S5_PAIR):
        for b in range(bsz):
            y_ref[b, :, q * LANES:(q + 1) * LANES] = ybuf_ref[q, pl.ds(b, tt, stride=bsz), :]


def _s5(u3, w1, cbd, apow, dskip, *, tt=512):
    bsz, seq, _ = u3.shape
    n_sub = SUBLANES // bsz
    tt = min(tt, seq)
    m = bsz * tt
    width = S5_PAIR * LANES
    ns = S5_PAIR * S5_BLOCK_STATE
    blk = pl.BlockSpec((bsz, tt, width), lambda j, i: (0, i, j))
    return pl.pallas_call(
        functools.partial(_s5_body, n_sub=n_sub),
        grid=(S5_BLOCKS // S5_PAIR, seq // tt),
        in_specs=[blk,
                  pl.BlockSpec((S5_PAIR, 2, n_sub * LANES, S5_BLOCK_STATE), lambda j, i: (j, 0, 0, 0)),
                  pl.BlockSpec((1, 2 * ns, width), lambda j, i: (j, 0, 0)),
                  pl.BlockSpec((1, 2, ns), lambda j, i: (j, 0, 0)),
                  pl.BlockSpec((1, width), lambda j, i: (0, j))],
        out_specs=blk,
        out_shape=jax.ShapeDtypeStruct((bsz, seq, D_MODEL), F32),
        scratch_shapes=[pltpu.VMEM((S5_PAIR, m + SUBLANES, LANES), F32),
                        pltpu.VMEM((m, 2 * ns), F32),
                        pltpu.VMEM((m, 2 * ns), BF16),
                        pltpu.VMEM((SUBLANES, 2 * ns), F32),
                        pltpu.VMEM((S5_PAIR, m, LANES), F32)],
        compiler_params=_params(("parallel", "arbitrary")),
        name="s5",
    )(u3, w1, cbd, apow, dskip)


def _s5_operators(a_re, a_im, log_dt, b_re, b_im, c_re, c_im, n_sub):
    dt = jnp.exp(log_dt)[:, None]
    lr, li = a_re, a_im
    mag = jnp.exp(lr * dt)
    ar, ai = mag * jnp.cos(li * dt), mag * jnp.sin(li * dt)
    den = lr * lr + li * li
    cr = ((ar - 1.0) * lr + ai * li) / den
    ci = (ai * lr - (ar - 1.0) * li) / den
    bbr = cr[..., None] * b_re - ci[..., None] * b_im
    bbi = cr[..., None] * b_im + ci[..., None] * b_re

    def power(k):
        m = jnp.exp(k * lr * dt)
        return m * jnp.cos(k * li * dt), m * jnp.sin(k * li * dt)

    eye = jnp.eye(S5_BLOCK_GROUPS, dtype=F32)

    def blockdiag_in(w):
        w = w.reshape(S5_BLOCKS, S5_BLOCK_GROUPS, S5_STATE, S5_GROUP)
        return jnp.einsum('jgnm,gh->jgmhn', w, eye).reshape(S5_BLOCKS, LANES, S5_BLOCK_STATE)

    def blockdiag_out(c):
        c = c.reshape(S5_BLOCKS, S5_BLOCK_GROUPS, S5_GROUP, S5_STATE)
        return jnp.einsum('jgmn,gh->jgnhm', c, eye).reshape(S5_BLOCKS, S5_BLOCK_STATE, LANES)

    rows = []
    for k in range(n_sub):
        pr, pi = power(float(k))
        wr = pr[..., None] * bbr - pi[..., None] * bbi
        wi = pr[..., None] * bbi + pi[..., None] * bbr
        rows.append(jnp.stack([blockdiag_in(wr), blockdiag_in(wi)], axis=1))
    w1 = jnp.concatenate(rows, axis=2).astype(BF16)
    eye_p = jnp.eye(S5_PAIR, dtype=F32)

    def pair_out(c):
        c = c.reshape(S5_BLOCKS // S5_PAIR, S5_PAIR, S5_BLOCK_STATE, LANES)
        return jnp.einsum('jqnm,qr->jqnrm', c, eye_p).reshape(S5_BLOCKS // S5_PAIR, S5_PAIR * S5_BLOCK_STATE,
                                                              S5_PAIR * LANES)

    cbd = jnp.concatenate([pair_out(blockdiag_out(c_re)), pair_out(blockdiag_out(-c_im))], axis=1).astype(BF16)
    sr, si = power(float(n_sub))
    apow = jnp.stack([sr.reshape(S5_BLOCKS // S5_PAIR, S5_PAIR * S5_BLOCK_STATE),
                      si.reshape(S5_BLOCKS // S5_PAIR, S5_PAIR * S5_BLOCK_STATE)], axis=1)
    return w1, cbd, apow


def _ssd_chunk(xbc, dt_full, alog_ref, dexp_ref, e_ref, tri_ref, causal_ref, diag_ref, h_ref):
    lc = SSD_CHUNK
    xs_b = xbc[:, :D_INNER]
    xs = xs_b.astype(F32)
    bm = xbc[:, D_INNER:D_INNER + BC_WIDTH]
    cm = xbc[:, D_INNER + BC_WIDTH:]

    dt = dt_full[:, :HEADS]
    da = dt * (-jnp.exp(alog_ref[...]))
    a_cum = _dot(tri_ref[...], jnp.concatenate(_split3(da), axis=0))
    both = jnp.concatenate([a_cum, dt], axis=0)
    terms = jnp.concatenate([v.astype(F32) for v in _split3(both)], axis=1).astype(BF16)
    expanded = _dot(terms, e_ref[...])
    a_e = expanded[:lc]
    dt_e = expanded[lc:]
    diag = diag_ref[...]
    a_j = jnp.sum(a_e * diag, axis=0, keepdims=True)
    dt_j = jnp.sum(dt_e * diag, axis=0, keepdims=True)
    lmat = jnp.exp(a_e - a_j + causal_ref[...])
    a_last = a_e[lc - 1:lc, :]
    exp_a = jnp.exp(a_e)
    xw = (xs * (jnp.exp(a_last - a_e) * dt_e)).astype(BF16)
    chunk_decay = jnp.exp(a_last)

    blk_r = lax.broadcasted_iota(jnp.int32, (GROUP_COLS, GROUP_COLS), 0) // HEADDIM
    blk_c = lax.broadcasted_iota(jnp.int32, (GROUP_COLS, GROUP_COLS), 1) // HEADDIM
    same_head = blk_r == blk_c
    nt = (((1,), (1,)), ((), ()))
    tn = (((0,), (0,)), ((), ()))
    ys = []
    for g in range(SSD_GROUPS):
        cs = slice(g * GROUP_COLS, (g + 1) * GROUP_COLS)
        ns = slice(g * SSD_STATE, (g + 1) * SSD_STATE)
        bm_g, cm_g = bm[:, ns], cm[:, ns]
        cb = lax.dot_general(cm_g, jnp.concatenate([bm_g] * HPG, axis=0), nt,
                             preferred_element_type=F32)
        w = (cb * lmat[:, cs] * dt_j[:, cs]).astype(BF16)
        x_g = xs_b[:, cs]
        x_bd = jnp.where(same_head, jnp.concatenate([x_g] * HPG, axis=0), jnp.zeros((), BF16))
        y_diag = _dot(w, x_bd)
        h_prev = h_ref[g]
        y_off = _dot(cm_g, h_prev.astype(BF16)) * exp_a[:, cs]
        ys.append(y_diag + y_off)
        st = lax.dot_general(bm_g, xw[:, cs], tn, preferred_element_type=F32)
        h_ref[g] = h_prev * chunk_decay[:, cs] + st

    return (jnp.concatenate(ys, axis=1) + dexp_ref[...] * xs).astype(BF16)


def _ssd_body(xbc_ref, dt_ref, alog_ref, dexp_ref, e_ref, tri_ref, causal_ref, diag_ref, y_ref, h_ref, *, cps):
    lc = SSD_CHUNK

    @pl.when(pl.program_id(1) == 0)
    def _():
        h_ref[...] = jnp.zeros(h_ref.shape, F32)

    for ci in range(cps):
        rs = slice(ci * lc, (ci + 1) * lc)
        y_ref[rs, :] = _ssd_chunk(xbc_ref[rs, :], dt_ref[rs, :], alog_ref, dexp_ref, e_ref, tri_ref,
                                  causal_ref, diag_ref, h_ref)


def _ssd(xbc, dt, alog, dexp, e, tri, causal, diag, *, batch, cps=4):
    t = xbc.shape[0]
    seq = t // batch
    cps = min(cps, seq // SSD_CHUNK)
    rows = cps * SSD_CHUNK
    ns = seq // rows

    def tile(c):
        return pl.BlockSpec((rows, c), lambda b, i: (b * ns + i, 0))

    return pl.pallas_call(
        functools.partial(_ssd_body, cps=cps),
        grid=(batch, ns),
        in_specs=[tile(CONV_DIM), tile(DT_PAD)] + [_whole()] * 6,
        out_specs=tile(D_INNER),
        out_shape=jax.ShapeDtypeStruct((t, D_INNER), BF16),
        scratch_shapes=[pltpu.VMEM((SSD_GROUPS, SSD_STATE, GROUP_COLS), F32)],
        compiler_params=_params(("parallel", "arbitrary")),
        name="ssd",
    )(xbc, dt, alog, dexp, e, tri, causal, diag)


def _ssd_constants():
    r = np.arange(SSD_CHUNK)
    time = SUBLANES * (r % SUBLANES) + r // SUBLANES
    before = time[None, :] <= time[:, None]
    tri = np.tile(before.astype(np.float32), (1, 3))
    causal = np.where(np.tile(before, (1, HEADS)), 0.0, -np.inf).astype(np.float32)
    diag = np.tile(np.eye(SSD_CHUNK, dtype=np.float32), (1, HEADS))
    e_mat = np.tile(np.repeat(np.eye(HEADS, dtype=np.float32), HEADDIM, axis=1), (3, 1))
    return (jnp.asarray(e_mat, BF16), jnp.asarray(tri, BF16), jnp.asarray(causal), jnp.asarray(diag))


def _merge_body(x_ref, y5_ref, yssd_ref, gm_ref, wz_ref, wgt_ref, nw_ref, bg_ref, wglu_ref, bglu_ref,
                wp5_ref, wpssd_ref, wout_ref, o_ref):
    x = x_ref[...]
    h = (x * _rms_scale(x) * gm_ref[...]).astype(BF16)
    y = _block_transpose(y5_ref[...])
    gl = 0.5 * y * (1.0 + jnp.tanh(np.float32(np.sqrt(2.0 / np.pi)) * (y + 0.044715 * (y * y * y))))
    glu = gl * _sigmoid(_dot(gl.astype(BF16), wglu_ref[...]) + bglu_ref[...])
    p5 = _dot(glu.astype(BF16), wp5_ref[...])
    t = yssd_ref[...].astype(F32) * _silu(_dot(h, wz_ref[...]))
    tn = (t * _rms_scale(t) * nw_ref[...]).astype(BF16)
    pssd = _dot(tn, wpssd_ref[...])
    g5 = _sigmoid(_dot(h, wgt_ref[:, :D_MODEL]) + bg_ref[:, :D_MODEL])
    gssd = _sigmoid(_dot(h, wgt_ref[:, D_MODEL:]) + bg_ref[:, D_MODEL:])
    merged = g5 * p5 + gssd * pssd
    o_ref[...] = x + _dot(merged.astype(BF16), wout_ref[...])


def _merge(x, y5, yssd, gm, wz, wgt, nw, bg, wglu, bglu, wp5, wpssd, wout, *, tm=256):
    t = x.shape[0]
    tm = min(tm, t)

    def tile(c):
        return pl.BlockSpec((tm, c), lambda i: (i, 0))

    return pl.pallas_call(
        _merge_body,
        grid=(t // tm,),
        in_specs=[tile(D_MODEL), tile(D_MODEL), tile(D_INNER)] + [_whole()] * 10,
        out_specs=tile(D_MODEL),
        out_shape=jax.ShapeDtypeStruct((t, D_MODEL), F32),
        compiler_params=_params(("parallel",)),
        name="merge",
    )(x, y5, yssd, gm, wz, wgt, nw, bg, wglu, bglu, wp5, wpssd, wout)


def kernel(x, ffn1_norm, ffn1_w_gate, ffn1_w_up, ffn1_w_down, mix_norm, w_in, conv_w, conv_b, s5_A_re, s5_A_im, s5_log_dt, s5_B_re, s5_B_im, s5_C_re, s5_C_im, s5_D, s5_w_glu, s5_b_glu, ssd_A_log, ssd_dt_bias, ssd_D, ssd_norm, w_proj_s5, w_proj_ssd, b_gate, w_out, ffn2_norm, ffn2_w_gate, ffn2_w_up, ffn2_w_down, final_norm):
    bsz, seq, _ = x.shape
    assert SUBLANES % bsz == 0, "the S5 row interleave needs the batch to divide the sublane count"
    assert seq % ROW_BLOCK == 0
    t = bsz * seq
    depth = ffn1_norm.shape[0]
    xt = x.reshape(t, D_MODEL)
    row = lambda v: v.reshape(1, -1).astype(F32)
    fin = row(final_norm)
    e_mat, tri, causal, diag = _ssd_constants()
    dt_end = MAIN_COLS + HEADS

    for i in range(depth):
        xt = _ffn(xt, row(ffn1_norm[i]), ffn1_w_gate[i].astype(BF16), ffn1_w_up[i].astype(BF16),
                  ffn1_w_down[i].astype(BF16), fin, first=(i == 0), final=False)

        w = w_in[i]
        w_dt = jnp.pad(w[:, MAIN_COLS:dt_end], ((0, 0), (0, DT_PAD - HEADS))).astype(BF16)
        dtb = jnp.pad(ssd_dt_bias[i], (0, DT_PAD - HEADS)).reshape(1, DT_PAD)
        z_end = D_MODEL + D_INNER
        u, xbc, dt = _inproj(xt, row(mix_norm[i]), w[:, :D_MODEL].astype(BF16), w[:, z_end:MAIN_COLS].astype(BF16),
                             w_dt, conv_w[i], row(conv_b[i]), dtb, batch=bsz)

        w1, cbd, apow = _s5_operators(s5_A_re[i], s5_A_im[i], s5_log_dt[i], s5_B_re[i], s5_B_im[i],
                                      s5_C_re[i], s5_C_im[i], SUBLANES // bsz)
        y5 = _s5(u.reshape(bsz, seq, D_MODEL), w1, cbd, apow, row(s5_D[i])).reshape(t, D_MODEL)

        yssd = _ssd(xbc, dt, row(ssd_A_log[i]), row(jnp.repeat(ssd_D[i], HEADDIM)), e_mat, tri, causal, diag,
                    batch=bsz)

        xt = _merge(xt, y5, yssd, row(mix_norm[i]), w[:, D_MODEL:z_end].astype(BF16), w[:, dt_end:].astype(BF16),
                    row(ssd_norm[i]), row(b_gate[i]), s5_w_glu[i].astype(BF16),
                    row(s5_b_glu[i]), w_proj_s5[i].astype(BF16), w_proj_ssd[i].astype(BF16),
                    w_out[i].astype(BF16))

        xt = _ffn(xt, row(ffn2_norm[i]), ffn2_w_gate[i].astype(BF16), ffn2_w_up[i].astype(BF16),
                  ffn2_w_down[i].astype(BF16), fin, first=False, final=(i == depth - 1))
    return xt.reshape(bsz, seq, D_MODEL)
```

```python
import functools

import jax
import jax.numpy as jnp
import numpy as np
from jax import lax
from jax.experimental import pallas as pl
from jax.experimental.pallas import tpu as pltpu

EPS = 1e-6
D_MODEL = 1024
D_FF = 2816
S5_GROUP = 16
S5_GROUPS = 64
S5_STATE = 64
D_INNER = 2048
HEADDIM = 64
HEADS = 32
SSD_GROUPS = 8
HPG = 4
SSD_STATE = 128
CONV_K = 4
CONV_DIM = 4096
BC_WIDTH = SSD_GROUPS * SSD_STATE
GROUP_COLS = HPG * HEADDIM

LANES = 128
SUBLANES = 8
ROW_BLOCK = SUBLANES * SUBLANES
S5_BLOCK_GROUPS = LANES // S5_GROUP
S5_BLOCKS = S5_GROUPS // S5_BLOCK_GROUPS
S5_BLOCK_STATE = S5_BLOCK_GROUPS * S5_STATE
S5_PAIR = 2
S5_SUB_ROWS = 512
SSD_CHUNK = ROW_BLOCK
DT_PAD = LANES
PROJ_COLS = 512
FF_COLS = 256
MAIN_COLS = D_MODEL + D_INNER + CONV_DIM

VMEM_LIMIT = 56 * 1024 * 1024

F32 = jnp.float32
BF16 = jnp.bfloat16


def _dot(a, b):
    return jnp.dot(a, b, preferred_element_type=F32)


def _sigmoid(v):
    return 1.0 / (1.0 + jnp.exp(-v))


def _silu(v):
    return v * _sigmoid(v)


def _rms_scale(xf):
    return lax.rsqrt(jnp.mean(xf * xf, axis=-1, keepdims=True) + EPS)


def _split3(v):
    hi = v.astype(BF16)
    r1 = v - hi.astype(F32)
    mid = r1.astype(BF16)
    lo = (r1 - mid.astype(F32)).astype(BF16)
    return hi, mid, lo


def _block_transpose(v):
    rows, cols = v.shape
    return jnp.swapaxes(v.reshape(rows // ROW_BLOCK, SUBLANES, SUBLANES, cols), 1, 2).reshape(rows, cols)


def _whole():
    return pl.BlockSpec(memory_space=pltpu.VMEM)


def _params(sem):
    return pltpu.CompilerParams(dimension_semantics=sem, vmem_limit_bytes=VMEM_LIMIT)


def _ffn_body(x_ref, g_ref, wg_ref, wu_ref, wd_ref, fin_ref, o_ref, *, first, final):
    x = _block_transpose(x_ref[...]) if first else x_ref[...]
    h = (x * _rms_scale(x) * g_ref[...]).astype(BF16)
    acc = jnp.zeros(x.shape, F32)
    for c0 in range(0, D_FF, FF_COLS):
        sl = slice(c0, c0 + FF_COLS)
        a = _dot(h, wg_ref[:, sl].astype(BF16))
        b = _dot(h, wu_ref[:, sl].astype(BF16))
        act = (_silu(a) * b).astype(BF16)
        acc = acc + _dot(act, wd_ref[sl, :].astype(BF16))
    y = x + 0.5 * acc
    if final:
        y = y * _rms_scale(y) * fin_ref[...]
        o_ref[...] = _block_transpose(y)
    else:
        o_ref[...] = y


def _ffn(x, g, wg, wu, wd, fin, *, first, final, tm=512):
    t = x.shape[0]
    tm = min(tm, t)
    tile = pl.BlockSpec((tm, D_MODEL), lambda i: (i, 0))
    return pl.pallas_call(
        functools.partial(_ffn_body, first=first, final=final),
        grid=(t // tm,),
        in_specs=[tile, _whole(), _whole(), _whole(), _whole(), _whole()],
        out_specs=tile,
        out_shape=jax.ShapeDtypeStruct((t, D_MODEL), F32),
        compiler_params=_params(("parallel",)),
        name="ffn_final" if final else "ffn",
    )(x, g, wg, wu, wd, fin)


def _inproj_body(x_ref, g_ref, wu_ref, wx_ref, wdt_ref, cw_ref, cb_ref, dtb_ref,
                 u_ref, xbc_ref, dt_ref, tail_ref):
    tm = x_ref.shape[0]
    nblk = tm // ROW_BLOCK
    x = x_ref[...]
    h = (x * _rms_scale(x) * g_ref[...]).astype(BF16)
    for c0 in range(0, D_MODEL, PROJ_COLS):
        u_ref[:, c0:c0 + PROJ_COLS] = _block_transpose(_dot(h, wu_ref[:, c0:c0 + PROJ_COLS])).astype(BF16)
    raw = _dot(h, wdt_ref[...]) + dtb_ref[...]
    dt_ref[...] = jnp.maximum(raw, 0.0) + jnp.log1p(jnp.exp(-jnp.abs(raw)))

    @pl.when(pl.program_id(1) == 0)
    def _():
        tail_ref[...] = jnp.zeros(tail_ref.shape, F32)

    nd = CONV_K - 1
    sub = lax.broadcasted_iota(jnp.int32, (nblk, nd, SUBLANES, PROJ_COLS), 2)
    for c0 in range(0, CONV_DIM, PROJ_COLS):
        cs = slice(c0, c0 + PROJ_COLS)
        v = _dot(h, wx_ref[:, cs]).reshape(nblk, SUBLANES, SUBLANES, PROJ_COLS)
        rolled = pltpu.roll(v[:, SUBLANES - nd:].reshape(nblk * nd, SUBLANES, PROJ_COLS), 1, axis=1)
        rolled = rolled.reshape(nblk, nd, SUBLANES, PROJ_COLS)
        carry = tail_ref[:, cs].reshape(1, nd, SUBLANES, PROJ_COLS)
        tail_ref[:, cs] = rolled[nblk - 1].reshape(nd * SUBLANES, PROJ_COLS)
        prev = jnp.concatenate([carry, rolled[:nblk - 1]], axis=0) if nblk > 1 else carry
        edge = jnp.where(sub >= 1, rolled, prev)
        w_tap = lambda k: cw_ref[k:k + 1, cs].reshape(1, 1, 1, PROJ_COLS)
        acc = cb_ref[:, cs].reshape(1, 1, 1, PROJ_COLS) + w_tap(CONV_K - 1) * v
        for j in range(1, CONV_K):
            delayed = jnp.concatenate([edge[:, nd - j:], v[:, :SUBLANES - j]], axis=1)
            acc = acc + w_tap(CONV_K - 1 - j) * delayed
        xbc_ref[:, cs] = _silu(acc).reshape(tm, PROJ_COLS).astype(BF16)


def _inproj(x, g, wu, wx, wdt, cw, cb, dtb, *, batch, tm=512):
    t = x.shape[0]
    seq = t // batch
    tm = min(tm, seq)
    nt = seq // tm

    def tile(c):
        return pl.BlockSpec((tm, c), lambda b, i: (b * nt + i, 0))

    outs = ((D_MODEL, BF16), (CONV_DIM, BF16), (DT_PAD, F32))
    return pl.pallas_call(
        _inproj_body,
        grid=(batch, nt),
        in_specs=[tile(D_MODEL)] + [_whole()] * 7,
        out_specs=[tile(c) for c, _ in outs],
        out_shape=[jax.ShapeDtypeStruct((t, c), d) for c, d in outs],
        scratch_shapes=[pltpu.VMEM(((CONV_K - 1) * SUBLANES, CONV_DIM), F32)],
        compiler_params=_params(("parallel", "arbitrary")),
        name="inproj",
    )(x, g, wu, wx, wdt, cw, cb, dtb)


def _s5_body(u_ref, w1_ref, cbd_ref, apow_ref, dskip_ref, y_ref, ubuf_ref, bp_ref, xs_ref, x_ref, ybuf_ref,
             *, n_sub):
    bsz, tt, _ = u_ref.shape
    m = bsz * tt
    ns = S5_PAIR * S5_BLOCK_STATE
    i = pl.program_id(1)

    @pl.when(i == 0)
    def _():
        ubuf_ref[:, 0:SUBLANES, :] = jnp.zeros((S5_PAIR, SUBLANES, LANES), F32)
        x_ref[...] = jnp.zeros(x_ref.shape, F32)

    @pl.when(i > 0)
    def _():
        ubuf_ref[:, 0:SUBLANES, :] = ubuf_ref[:, m:m + SUBLANES, :]

    for q in range(S5_PAIR):
        for b in range(bsz):
            ubuf_ref[q, pl.ds(SUBLANES + b, tt, stride=bsz), :] = u_ref[b, :, q * LANES:(q + 1) * LANES].astype(F32)

    sub_rows = min(S5_SUB_ROWS, m)
    for r0 in range(0, m, sub_rows):
        for q in range(S5_PAIR):
            ucat = jnp.concatenate(
                [ubuf_ref[q, SUBLANES - k * bsz + r0:SUBLANES - k * bsz + r0 + sub_rows, :].astype(BF16)
                 for k in range(n_sub)], axis=1)
            bp_ref[r0:r0 + sub_rows, q * S5_BLOCK_STATE:(q + 1) * S5_BLOCK_STATE] = _dot(ucat, w1_ref[q, 0])
            bp_ref[r0:r0 + sub_rows, ns + q * S5_BLOCK_STATE:ns + (q + 1) * S5_BLOCK_STATE] = _dot(ucat, w1_ref[q, 1])

    a_r = jnp.broadcast_to(apow_ref[0, 0:1, :], (SUBLANES, ns))
    a_i = jnp.broadcast_to(apow_ref[0, 1:2, :], (SUBLANES, ns))
    pair = 2 * SUBLANES
    xr, xi = x_ref[:, 0:ns], x_ref[:, ns:2 * ns]
    for r in range(0, m, pair):
        re0 = a_r * xr - a_i * xi + bp_ref[r:r + SUBLANES, 0:ns]
        im0 = a_r * xi + a_i * xr + bp_ref[r:r + SUBLANES, ns:2 * ns]
        xr = a_r * re0 - a_i * im0 + bp_ref[r + SUBLANES:r + pair, 0:ns]
        xi = a_r * im0 + a_i * re0 + bp_ref[r + SUBLANES:r + pair, ns:2 * ns]
        xs_ref[r:r + pair, 0:ns] = jnp.concatenate([re0, xr], axis=0).astype(BF16)
        xs_ref[r:r + pair, ns:2 * ns] = jnp.concatenate([im0, xi], axis=0).astype(BF16)
    x_ref[:, 0:ns] = xr
    x_ref[:, ns:2 * ns] = xi

    for r0 in range(0, m, sub_rows):
        y = _dot(xs_ref[r0:r0 + sub_rows, :], cbd_ref[0])
        for q in range(S5_PAIR):
            cols = slice(q * LANES, (q + 1) * LANES)
            ybuf_ref[q, r0:r0 + sub_rows, :] = (
                y[:, cols] + dskip_ref[:, cols] * ubuf_ref[q, SUBLANES + r0:SUBLANES + r0 + sub_rows, :])
    for q in range(S5_PAIR):
        for b in range(bsz):
            y_ref[b, :, q * LANES:(q + 1) * LANES] = ybuf_ref[q, pl.ds(b, tt, stride=bsz), :]


def _s5(u3, w1, cbd, apow, dskip, *, tt=512):
    bsz, seq, _ = u3.shape
    n_sub = SUBLANES // bsz
    tt = min(tt, seq)
    m = bsz * tt
    width = S5_PAIR * LANES
    ns = S5_PAIR * S5_BLOCK_STATE
    blk = pl.BlockSpec((bsz, tt, width), lambda j, i: (0, i, j))
    return pl.pallas_call(
        functools.partial(_s5_body, n_sub=n_sub),
        grid=(S5_BLOCKS // S5_PAIR, seq // tt),
        in_specs=[blk,
                  pl.BlockSpec((S5_PAIR, 2, n_sub * LANES, S5_BLOCK_STATE), lambda j, i: (j, 0, 0, 0)),
                  pl.BlockSpec((1, 2 * ns, width), lambda j, i: (j, 0, 0)),
                  pl.BlockSpec((1, 2, ns), lambda j, i: (j, 0, 0)),
                  pl.BlockSpec((1, width), lambda j, i: (0, j))],
        out_specs=blk,
        out_shape=jax.ShapeDtypeStruct((bsz, seq, D_MODEL), F32),
        scratch_shapes=[pltpu.VMEM((S5_PAIR, m + SUBLANES, LANES), F32),
                        pltpu.VMEM((m, 2 * ns), F32),
                        pltpu.VMEM((m, 2 * ns), BF16),
                        pltpu.VMEM((SUBLANES, 2 * ns), F32),
                        pltpu.VMEM((S5_PAIR, m, LANES), F32)],
        compiler_params=_params(("parallel", "arbitrary")),
        name="s5",
    )(u3, w1, cbd, apow, dskip)


def _s5_operators(a_re, a_im, log_dt, b_re, b_im, c_re, c_im, n_sub):
    dt = jnp.exp(log_dt)[:, None]
    lr, li = a_re, a_im
    mag = jnp.exp(lr * dt)
    ar, ai = mag * jnp.cos(li * dt), mag * jnp.sin(li * dt)
    den = lr * lr + li * li
    cr = ((ar - 1.0) * lr + ai * li) / den
    ci = (ai * lr - (ar - 1.0) * li) / den
    bbr = cr[..., None] * b_re - ci[..., None] * b_im
    bbi = cr[..., None] * b_im + ci[..., None] * b_re

    def power(k):
        m = jnp.exp(k * lr * dt)
        return m * jnp.cos(k * li * dt), m * jnp.sin(k * li * dt)

    eye = jnp.eye(S5_BLOCK_GROUPS, dtype=F32)

    def blockdiag_in(w):
        w = w.reshape(S5_BLOCKS, S5_BLOCK_GROUPS, S5_STATE, S5_GROUP)
        return jnp.einsum('jgnm,gh->jgmhn', w, eye).reshape(S5_BLOCKS, LANES, S5_BLOCK_STATE)

    def blockdiag_out(c):
        c = c.reshape(S5_BLOCKS, S5_BLOCK_GROUPS, S5_GROUP, S5_STATE)
        return jnp.einsum('jgmn,gh->jgnhm', c, eye).reshape(S5_BLOCKS, S5_BLOCK_STATE, LANES)

    rows = []
    for k in range(n_sub):
        pr, pi = power(float(k))
        wr = pr[..., None] * bbr - pi[..., None] * bbi
        wi = pr[..., None] * bbi + pi[..., None] * bbr
        rows.append(jnp.stack([blockdiag_in(wr), blockdiag_in(wi)], axis=1))
    w1 = jnp.concatenate(rows, axis=2).astype(BF16)
    eye_p = jnp.eye(S5_PAIR, dtype=F32)

    def pair_out(c):
        c = c.reshape(S5_BLOCKS // S5_PAIR, S5_PAIR, S5_BLOCK_STATE, LANES)
        return jnp.einsum('jqnm,qr->jqnrm', c, eye_p).reshape(S5_BLOCKS // S5_PAIR, S5_PAIR * S5_BLOCK_STATE,
                                                              S5_PAIR * LANES)

    cbd = jnp.concatenate([pair_out(blockdiag_out(c_re)), pair_out(blockdiag_out(-c_im))], axis=1).astype(BF16)
    sr, si = power(float(n_sub))
    apow = jnp.stack([sr.reshape(S5_BLOCKS // S5_PAIR, S5_PAIR * S5_BLOCK_STATE),
                      si.reshape(S5_BLOCKS // S5_PAIR, S5_PAIR * S5_BLOCK_STATE)], axis=1)
    return w1, cbd, apow


def _ssd_chunk(xbc, dt_full, alog_ref, dexp_ref, e_ref, tri_ref, causal_ref, diag_ref, h_ref):
    lc = SSD_CHUNK
    xs_b = xbc[:, :D_INNER]
    xs = xs_b.astype(F32)
    bm = xbc[:, D_INNER:D_INNER + BC_WIDTH]
    cm = xbc[:, D_INNER + BC_WIDTH:]

    dt = dt_full[:, :HEADS]
    da = dt * (-jnp.exp(alog_ref[...]))
    a_cum = _dot(tri_ref[...], jnp.concatenate(_split3(da), axis=0))
    both = jnp.concatenate([a_cum, dt], axis=0)
    terms = jnp.concatenate([v.astype(F32) for v in _split3(both)], axis=1).astype(BF16)
    expanded = _dot(terms, e_ref[...])
    a_e = expanded[:lc]
    dt_e = expanded[lc:]
    diag = diag_ref[...]
    a_j = jnp.sum(a_e * diag, axis=0, keepdims=True)
    dt_j = jnp.sum(dt_e * diag, axis=0, keepdims=True)
    lmat = jnp.exp(a_e - a_j + causal_ref[...])
    a_last = a_e[lc - 1:lc, :]
    exp_a = jnp.exp(a_e)
    xw = (xs * (jnp.exp(a_last - a_e) * dt_e)).astype(BF16)
    chunk_decay = jnp.exp(a_last)

    blk_r = lax.broadcasted_iota(jnp.int32, (GROUP_COLS, GROUP_COLS), 0) // HEADDIM
    blk_c = lax.broadcasted_iota(jnp.int32, (GROUP_COLS, GROUP_COLS), 1) // HEADDIM
    same_head = blk_r == blk_c
    nt = (((1,), (1,)), ((), ()))
    tn = (((0,), (0,)), ((), ()))
    ys = []
    for g in range(SSD_GROUPS):
        cs = slice(g * GROUP_COLS, (g + 1) * GROUP_COLS)
        ns = slice(g * SSD_STATE, (g + 1) * SSD_STATE)
        bm_g, cm_g = bm[:, ns], cm[:, ns]
        cb = lax.dot_general(cm_g, jnp.concatenate([bm_g] * HPG, axis=0), nt,
                             preferred_element_type=F32)
        w = (cb * lmat[:, cs] * dt_j[:, cs]).astype(BF16)
        x_g = xs_b[:, cs]
        x_bd = jnp.where(same_head, jnp.concatenate([x_g] * HPG, axis=0), jnp.zeros((), BF16))
        y_diag = _dot(w, x_bd)
        h_prev = h_ref[g]
        y_off = _dot(cm_g, h_prev.astype(BF16)) * exp_a[:, cs]
        ys.append(y_diag + y_off)
        st = lax.dot_general(bm_g, xw[:, cs], tn, preferred_element_type=F32)
        h_ref[g] = h_prev * chunk_decay[:, cs] + st

    return (jnp.concatenate(ys, axis=1) + dexp_ref[...] * xs).astype(BF16)


def _ssd_body(xbc_ref, dt_ref, alog_ref, dexp_ref, e_ref, tri_ref, causal_ref, diag_ref, y_ref, h_ref, *, cps):
    lc = SSD_CHUNK

    @pl.when(pl.program_id(1) == 0)
    def _():
        h_ref[...] = jnp.zeros(h_ref.shape, F32)

    for ci in range(cps):
        rs = slice(ci * lc, (ci + 1) * lc)
        y_ref[rs, :] = _ssd_chunk(xbc_ref[rs, :], dt_ref[rs, :], alog_ref, dexp_ref, e_ref, tri_ref,
                                  causal_ref, diag_ref, h_ref)


def _ssd(xbc, dt, alog, dexp, e, tri, causal, diag, *, batch, cps=4):
    t = xbc.shape[0]
    seq = t // batch
    cps = min(cps, seq // SSD_CHUNK)
    rows = cps * SSD_CHUNK
    ns = seq // rows

    def tile(c):
        return pl.BlockSpec((rows, c), lambda b, i: (b * ns + i, 0))

    return pl.pallas_call(
        functools.partial(_ssd_body, cps=cps),
        grid=(batch, ns),
        in_specs=[tile(CONV_DIM), tile(DT_PAD)] + [_whole()] * 6,
        out_specs=tile(D_INNER),
        out_shape=jax.ShapeDtypeStruct((t, D_INNER), BF16),
        scratch_shapes=[pltpu.VMEM((SSD_GROUPS, SSD_STATE, GROUP_COLS), F32)],
        compiler_params=_params(("parallel", "arbitrary")),
        name="ssd",
    )(xbc, dt, alog, dexp, e, tri, causal, diag)


def _ssd_constants():
    r = np.arange(SSD_CHUNK)
    time = SUBLANES * (r % SUBLANES) + r // SUBLANES
    before = time[None, :] <= time[:, None]
    tri = np.tile(before.astype(np.float32), (1, 3))
    causal = np.where(np.tile(before, (1, HEADS)), 0.0, -np.inf).astype(np.float32)
    diag = np.tile(np.eye(SSD_CHUNK, dtype=np.float32), (1, HEADS))
    e_mat = np.tile(np.repeat(np.eye(HEADS, dtype=np.float32), HEADDIM, axis=1), (3, 1))
    return (jnp.asarray(e_mat, BF16), jnp.asarray(tri, BF16), jnp.asarray(causal), jnp.asarray(diag))


def _merge_body(x_ref, y5_ref, yssd_ref, gm_ref, wz_ref, wgt_ref, nw_ref, bg_ref, wglu_ref, bglu_ref,
                wp5_ref, wpssd_ref, wout_ref, o_ref):
    x = x_ref[...]
    h = (x * _rms_scale(x) * gm_ref[...]).astype(BF16)
    y = _block_transpose(y5_ref[...])
    gl = 0.5 * y * (1.0 + jnp.tanh(np.float32(np.sqrt(2.0 / np.pi)) * (y + 0.044715 * (y * y * y))))
    glu = gl * _sigmoid(_dot(gl.astype(BF16), wglu_ref[...].astype(BF16)) + bglu_ref[...])
    p5 = _dot(glu.astype(BF16), wp5_ref[...].astype(BF16))
    t = yssd_ref[...].astype(F32) * _silu(_dot(h, wz_ref[...]))
    tn = (t * _rms_scale(t) * nw_ref[...]).astype(BF16)
    pssd = _dot(tn, wpssd_ref[...].astype(BF16))
    g5 = _sigmoid(_dot(h, wgt_ref[:, :D_MODEL]) + bg_ref[:, :D_MODEL])
    gssd = _sigmoid(_dot(h, wgt_ref[:, D_MODEL:]) + bg_ref[:, D_MODEL:])
    merged = g5 * p5 + gssd * pssd
    o_ref[...] = x + _dot(merged.astype(BF16), wout_ref[...].astype(BF16))


def _merge(x, y5, yssd, gm, wz, wgt, nw, bg, wglu, bglu, wp5, wpssd, wout, *, tm=256):
    t = x.shape[0]
    tm = min(tm, t)

    def tile(c):
        return pl.BlockSpec((tm, c), lambda i: (i, 0))

    return pl.pallas_call(
        _merge_body,
        grid=(t // tm,),
        in_specs=[tile(D_MODEL), tile(D_MODEL), tile(D_INNER)] + [_whole()] * 10,
        out_specs=tile(D_MODEL),
        out_shape=jax.ShapeDtypeStruct((t, D_MODEL), F32),
        compiler_params=_params(("parallel",)),
        name="merge",
    )(x, y5, yssd, gm, wz, wgt, nw, bg, wglu, bglu, wp5, wpssd, wout)


def kernel(x, ffn1_norm, ffn1_w_gate, ffn1_w_up, ffn1_w_down, mix_norm, w_in, conv_w, conv_b, s5_A_re, s5_A_im, s5_log_dt, s5_B_re, s5_B_im, s5_C_re, s5_C_im, s5_D, s5_w_glu, s5_b_glu, ssd_A_log, ssd_dt_bias, ssd_D, ssd_norm, w_proj_s5, w_proj_ssd, b_gate, w_out, ffn2_norm, ffn2_w_gate, ffn2_w_up, ffn2_w_down, final_norm):
    bsz, seq, _ = x.shape
    assert SUBLANES % bsz == 0, "the S5 row interleave needs the batch to divide the sublane count"
    assert seq % ROW_BLOCK == 0
    t = bsz * seq
    depth = ffn1_norm.shape[0]
    xt = x.reshape(t, D_MODEL)
    row = lambda v: v.reshape(1, -1).astype(F32)
    fin = row(final_norm)
    e_mat, tri, causal, diag = _ssd_constants()
    dt_end = MAIN_COLS + HEADS

    for i in range(depth):
        xt = _ffn(xt, row(ffn1_norm[i]), ffn1_w_gate[i], ffn1_w_up[i], ffn1_w_down[i], fin,
                  first=(i == 0), final=False)

        w = w_in[i]
        w_dt = jnp.pad(w[:, MAIN_COLS:dt_end], ((0, 0), (0, DT_PAD - HEADS))).astype(BF16)
        dtb = jnp.pad(ssd_dt_bias[i], (0, DT_PAD - HEADS)).reshape(1, DT_PAD)
        z_end = D_MODEL + D_INNER
        u, xbc, dt = _inproj(xt, row(mix_norm[i]), w[:, :D_MODEL].astype(BF16), w[:, z_end:MAIN_COLS].astype(BF16),
                             w_dt, conv_w[i], row(conv_b[i]), dtb, batch=bsz)

        w1, cbd, apow = _s5_operators(s5_A_re[i], s5_A_im[i], s5_log_dt[i], s5_B_re[i], s5_B_im[i],
                                      s5_C_re[i], s5_C_im[i], SUBLANES // bsz)
        y5 = _s5(u.reshape(bsz, seq, D_MODEL), w1, cbd, apow, row(s5_D[i])).reshape(t, D_MODEL)

        yssd = _ssd(xbc, dt, row(ssd_A_log[i]), row(jnp.repeat(ssd_D[i], HEADDIM)), e_mat, tri, causal, diag,
                    batch=bsz)

        xt = _merge(xt, y5, yssd, row(mix_norm[i]), w[:, D_MODEL:z_end].astype(BF16), w[:, dt_end:].astype(BF16),
                    row(ssd_norm[i]), row(b_gate[i]), s5_w_glu[i], row(s5_b_glu[i]), w_proj_s5[i],
                    w_proj_ssd[i], w_out[i])

        xt = _ffn(xt, row(ffn2_norm[i]), ffn2_w_gate[i], ffn2_w_up[i], ffn2_w_down[i], fin,
                  first=False, final=(i == depth - 1))
    return xt.reshape(bsz, seq, D_MODEL)
```

```python
import functools

import jax
import jax.numpy as jnp
import numpy as np
from jax import lax
from jax.experimental import pallas as pl
from jax.experimental.pallas import tpu as pltpu

EPS = 1e-6
LOG2_E = float(np.log2(np.e))
D_MODEL = 1024
D_FF = 2816
S5_GROUP = 16
S5_GROUPS = 64
S5_STATE = 64
D_INNER = 2048
HEADDIM = 64
HEADS = 32
SSD_GROUPS = 8
HPG = 4
SSD_STATE = 128
CONV_K = 4
CONV_DIM = 4096
BC_WIDTH = SSD_GROUPS * SSD_STATE
GROUP_COLS = HPG * HEADDIM

LANES = 128
SUBLANES = 8
ROW_BLOCK = SUBLANES * SUBLANES
S5_BLOCK_GROUPS = LANES // S5_GROUP
S5_BLOCKS = S5_GROUPS // S5_BLOCK_GROUPS
S5_BLOCK_STATE = S5_BLOCK_GROUPS * S5_STATE
S5_PAIR = 2
S5_SUB_ROWS = 512
SSD_CHUNK = ROW_BLOCK
DT_PAD = LANES
PROJ_COLS = 512
FF_COLS = 256
MAIN_COLS = D_MODEL + D_INNER + CONV_DIM

VMEM_LIMIT = 56 * 1024 * 1024

F32 = jnp.float32
BF16 = jnp.bfloat16


def _dot(a, b):
    return jnp.dot(a, b, preferred_element_type=F32)


def _sigmoid(v):
    return 1.0 / (1.0 + jnp.exp(-v))


def _silu(v):
    return v * _sigmoid(v)


def _rms_scale(xf):
    return lax.rsqrt(jnp.mean(xf * xf, axis=-1, keepdims=True) + EPS)


def _split3(v):
    hi = v.astype(BF16)
    r1 = v - hi.astype(F32)
    mid = r1.astype(BF16)
    lo = (r1 - mid.astype(F32)).astype(BF16)
    return hi, mid, lo


def _block_transpose(v):
    rows, cols = v.shape
    return jnp.swapaxes(v.reshape(rows // ROW_BLOCK, SUBLANES, SUBLANES, cols), 1, 2).reshape(rows, cols)


def _whole():
    return pl.BlockSpec(memory_space=pltpu.VMEM)


def _params(sem):
    return pltpu.CompilerParams(dimension_semantics=sem, vmem_limit_bytes=VMEM_LIMIT)


def _ffn_body(x_ref, g_ref, wg_ref, wu_ref, wd_ref, fin_ref, o_ref, *, first, final):
    x = _block_transpose(x_ref[...]) if first else x_ref[...]
    h = (x * _rms_scale(x) * g_ref[...]).astype(BF16)
    acc = jnp.zeros(x.shape, F32)
    for c0 in range(0, D_FF, FF_COLS):
        sl = slice(c0, c0 + FF_COLS)
        a = _dot(h, wg_ref[:, sl].astype(BF16))
        b = _dot(h, wu_ref[:, sl].astype(BF16))
        act = (_silu(a) * b).astype(BF16)
        acc = acc + _dot(act, wd_ref[sl, :].astype(BF16))
    y = x + 0.5 * acc
    if final:
        y = y * _rms_scale(y) * fin_ref[...]
        o_ref[...] = _block_transpose(y)
    else:
        o_ref[...] = y


def _ffn(x, g, wg, wu, wd, fin, *, first, final, tm=512):
    t = x.shape[0]
    tm = min(tm, t)
    tile = pl.BlockSpec((tm, D_MODEL), lambda i: (i, 0))
    return pl.pallas_call(
        functools.partial(_ffn_body, first=first, final=final),
        grid=(t // tm,),
        in_specs=[tile, _whole(), _whole(), _whole(), _whole(), _whole()],
        out_specs=tile,
        out_shape=jax.ShapeDtypeStruct((t, D_MODEL), F32),
        compiler_params=_params(("parallel",)),
        name="ffn_final" if final else "ffn",
    )(x, g, wg, wu, wd, fin)


def _inproj_body(x_ref, g_ref, wu_ref, wx_ref, wdt_ref, cw_ref, cb_ref, dtb_ref,
                 u_ref, xbc_ref, dt_ref, tail_ref):
    tm = x_ref.shape[0]
    nblk = tm // ROW_BLOCK
    x = x_ref[...]
    h = (x * _rms_scale(x) * g_ref[...]).astype(BF16)
    for c0 in range(0, D_MODEL, PROJ_COLS):
        u_ref[:, c0:c0 + PROJ_COLS] = _block_transpose(_dot(h, wu_ref[:, c0:c0 + PROJ_COLS])).astype(BF16)
    raw = _dot(h, wdt_ref[...]) + dtb_ref[...]
    dt_ref[...] = jnp.maximum(raw, 0.0) + jnp.log1p(jnp.exp(-jnp.abs(raw)))

    @pl.when(pl.program_id(1) == 0)
    def _():
        tail_ref[...] = jnp.zeros(tail_ref.shape, F32)

    nd = CONV_K - 1
    sub = lax.broadcasted_iota(jnp.int32, (nblk, nd, SUBLANES, PROJ_COLS), 2)
    for c0 in range(0, CONV_DIM, PROJ_COLS):
        cs = slice(c0, c0 + PROJ_COLS)
        v = _dot(h, wx_ref[:, cs]).reshape(nblk, SUBLANES, SUBLANES, PROJ_COLS)
        rolled = pltpu.roll(v[:, SUBLANES - nd:].reshape(nblk * nd, SUBLANES, PROJ_COLS), 1, axis=1)
        rolled = rolled.reshape(nblk, nd, SUBLANES, PROJ_COLS)
        carry = tail_ref[:, cs].reshape(1, nd, SUBLANES, PROJ_COLS)
        tail_ref[:, cs] = rolled[nblk - 1].reshape(nd * SUBLANES, PROJ_COLS)
        prev = jnp.concatenate([carry, rolled[:nblk - 1]], axis=0) if nblk > 1 else carry
        edge = jnp.where(sub >= 1, rolled, prev)
        w_tap = lambda k: cw_ref[k:k + 1, cs].reshape(1, 1, 1, PROJ_COLS)
        acc = cb_ref[:, cs].reshape(1, 1, 1, PROJ_COLS) + w_tap(CONV_K - 1) * v
        for j in range(1, CONV_K):
            delayed = jnp.concatenate([edge[:, nd - j:], v[:, :SUBLANES - j]], axis=1)
            acc = acc + w_tap(CONV_K - 1 - j) * delayed
        xbc_ref[:, cs] = _silu(acc).reshape(tm, PROJ_COLS).astype(BF16)


def _inproj(x, g, wu, wx, wdt, cw, cb, dtb, *, batch, tm=512):
    t = x.shape[0]
    seq = t // batch
    tm = min(tm, seq)
    nt = seq // tm

    def tile(c):
        return pl.BlockSpec((tm, c), lambda b, i: (b * nt + i, 0))

    outs = ((D_MODEL, BF16), (CONV_DIM, BF16), (DT_PAD, F32))
    return pl.pallas_call(
        _inproj_body,
        grid=(batch, nt),
        in_specs=[tile(D_MODEL)] + [_whole()] * 7,
        out_specs=[tile(c) for c, _ in outs],
        out_shape=[jax.ShapeDtypeStruct((t, c), d) for c, d in outs],
        scratch_shapes=[pltpu.VMEM(((CONV_K - 1) * SUBLANES, CONV_DIM), F32)],
        compiler_params=_params(("parallel", "arbitrary")),
        name="inproj",
    )(x, g, wu, wx, wdt, cw, cb, dtb)


def _s5_body(u_ref, w1_ref, cbd_ref, apow_ref, dskip_ref, y_ref, ubuf_ref, bp_ref, xs_ref, x_ref, ybuf_ref,
             *, n_sub):
    bsz, tt, _ = u_ref.shape
    m = bsz * tt
    ns = S5_PAIR * S5_BLOCK_STATE
    i = pl.program_id(1)

    @pl.when(i == 0)
    def _():
        ubuf_ref[:, 0:SUBLANES, :] = jnp.zeros((S5_PAIR, SUBLANES, LANES), F32)
        x_ref[...] = jnp.zeros(x_ref.shape, F32)

    @pl.when(i > 0)
    def _():
        ubuf_ref[:, 0:SUBLANES, :] = ubuf_ref[:, m:m + SUBLANES, :]

    for q in range(S5_PAIR):
        for b in range(bsz):
            ubuf_ref[q, pl.ds(SUBLANES + b, tt, stride=bsz), :] = u_ref[b, :, q * LANES:(q + 1) * LANES].astype(F32)

    sub_rows = min(S5_SUB_ROWS, m)
    for r0 in range(0, m, sub_rows):
        for q in range(S5_PAIR):
            ucat = jnp.concatenate(
                [ubuf_ref[q, SUBLANES - k * bsz + r0:SUBLANES - k * bsz + r0 + sub_rows, :].astype(BF16)
                 for k in range(n_sub)], axis=1)
            bp_ref[r0:r0 + sub_rows, q * S5_BLOCK_STATE:(q + 1) * S5_BLOCK_STATE] = _dot(ucat, w1_ref[q, 0])
            bp_ref[r0:r0 + sub_rows, ns + q * S5_BLOCK_STATE:ns + (q + 1) * S5_BLOCK_STATE] = _dot(ucat, w1_ref[q, 1])

    a_r = jnp.broadcast_to(apow_ref[0, 0:1, :], (SUBLANES, ns))
    a_i = jnp.broadcast_to(apow_ref[0, 1:2, :], (SUBLANES, ns))
    pair = 2 * SUBLANES
    xr, xi = x_ref[:, 0:ns], x_ref[:, ns:2 * ns]
    for r in range(0, m, pair):
        re0 = a_r * xr - a_i * xi + bp_ref[r:r + SUBLANES, 0:ns]
        im0 = a_r * xi + a_i * xr + bp_ref[r:r + SUBLANES, ns:2 * ns]
        xr = a_r * re0 - a_i * im0 + bp_ref[r + SUBLANES:r + pair, 0:ns]
        xi = a_r * im0 + a_i * re0 + bp_ref[r + SUBLANES:r + pair, ns:2 * ns]
        xs_ref[r:r + pair, 0:ns] = jnp.concatenate([re0, xr], axis=0).astype(BF16)
        xs_ref[r:r + pair, ns:2 * ns] = jnp.concatenate([im0, xi], axis=0).astype(BF16)
    x_ref[:, 0:ns] = xr
    x_ref[:, ns:2 * ns] = xi

    for r0 in range(0, m, sub_rows):
        y = _dot(xs_ref[r0:r0 + sub_rows, :], cbd_ref[0])
        for q in range(S5_PAIR):
            cols = slice(q * LANES, (q + 1) * LANES)
            ybuf_ref[q, r0:r0 + sub_rows, :] = (
                y[:, cols] + dskip_ref[:, cols] * ubuf_ref[q, SUBLANES + r0:SUBLANES + r0 + sub_rows, :])
    for q in range(S5_PAIR):
        for b in range(bsz):
            y_ref[b, :, q * LANES:(q + 1) * LANES] = ybuf_ref[q, pl.ds(b, tt, stride=bsz), :]


def _s5(u3, w1, cbd, apow, dskip, *, tt=512):
    bsz, seq, _ = u3.shape
    n_sub = SUBLANES // bsz
    tt = min(tt, seq)
    m = bsz * tt
    width = S5_PAIR * LANES
    ns = S5_PAIR * S5_BLOCK_STATE
    blk = pl.BlockSpec((bsz, tt, width), lambda j, i: (0, i, j))
    return pl.pallas_call(
        functools.partial(_s5_body, n_sub=n_sub),
        grid=(S5_BLOCKS // S5_PAIR, seq // tt),
        in_specs=[blk,
                  pl.BlockSpec((S5_PAIR, 2, n_sub * LANES, S5_BLOCK_STATE), lambda j, i: (j, 0, 0, 0)),
                  pl.BlockSpec((1, 2 * ns, width), lambda j, i: (j, 0, 0)),
                  pl.BlockSpec((1, 2, ns), lambda j, i: (j, 0, 0)),
                  pl.BlockSpec((1, width), lambda j, i: (0, j))],
        out_specs=blk,
        out_shape=jax.ShapeDtypeStruct((bsz, seq, D_MODEL), F32),
        scratch_shapes=[pltpu.VMEM((S5_PAIR, m + SUBLANES, LANES), F32),
                        pltpu.VMEM((m, 2 * ns), F32),
                        pltpu.VMEM((m, 2 * ns), BF16),
                        pltpu.VMEM((SUBLANES, 2 * ns), F32),
                        pltpu.VMEM((S5_PAIR, m, LANES), F32)],
        compiler_params=_params(("parallel", "arbitrary")),
        name="s5",
    )(u3, w1, cbd, apow, dskip)


def _s5_operators(a_re, a_im, log_dt, b_re, b_im, c_re, c_im, n_sub):
    dt = jnp.exp(log_dt)[:, None]
    lr, li = a_re, a_im
    mag = jnp.exp(lr * dt)
    ar, ai = mag * jnp.cos(li * dt), mag * jnp.sin(li * dt)
    den = lr * lr + li * li
    cr = ((ar - 1.0) * lr + ai * li) / den
    ci = (ai * lr - (ar - 1.0) * li) / den
    bbr = cr[..., None] * b_re - ci[..., None] * b_im
    bbi = cr[..., None] * b_im + ci[..., None] * b_re

    def power(k):
        m = jnp.exp(k * lr * dt)
        return m * jnp.cos(k * li * dt), m * jnp.sin(k * li * dt)

    eye = jnp.eye(S5_BLOCK_GROUPS, dtype=F32)

    def blockdiag_in(w):
        w = w.reshape(S5_BLOCKS, S5_BLOCK_GROUPS, S5_STATE, S5_GROUP)
        return jnp.einsum('jgnm,gh->jgmhn', w, eye).reshape(S5_BLOCKS, LANES, S5_BLOCK_STATE)

    def blockdiag_out(c):
        c = c.reshape(S5_BLOCKS, S5_BLOCK_GROUPS, S5_GROUP, S5_STATE)
        return jnp.einsum('jgmn,gh->jgnhm', c, eye).reshape(S5_BLOCKS, S5_BLOCK_STATE, LANES)

    rows = []
    for k in range(n_sub):
        pr, pi = power(float(k))
        wr = pr[..., None] * bbr - pi[..., None] * bbi
        wi = pr[..., None] * bbi + pi[..., None] * bbr
        rows.append(jnp.stack([blockdiag_in(wr), blockdiag_in(wi)], axis=1))
    w1 = jnp.concatenate(rows, axis=2).astype(BF16)
    eye_p = jnp.eye(S5_PAIR, dtype=F32)

    def pair_out(c):
        c = c.reshape(S5_BLOCKS // S5_PAIR, S5_PAIR, S5_BLOCK_STATE, LANES)
        return jnp.einsum('jqnm,qr->jqnrm', c, eye_p).reshape(S5_BLOCKS // S5_PAIR, S5_PAIR * S5_BLOCK_STATE,
                                                              S5_PAIR * LANES)

    cbd = jnp.concatenate([pair_out(blockdiag_out(c_re)), pair_out(blockdiag_out(-c_im))], axis=1).astype(BF16)
    sr, si = power(float(n_sub))
    apow = jnp.stack([sr.reshape(S5_BLOCKS // S5_PAIR, S5_PAIR * S5_BLOCK_STATE),
                      si.reshape(S5_BLOCKS // S5_PAIR, S5_PAIR * S5_BLOCK_STATE)], axis=1)
    return w1, cbd, apow


def _ssd_chunk(xbc, dt_full, alog_ref, dexp_ref, e_ref, tri_ref, causal_ref, diag_ref, h_ref):
    lc = SSD_CHUNK
    xs_b = xbc[:, :D_INNER]
    xs = xs_b.astype(F32)
    bm = xbc[:, D_INNER:D_INNER + BC_WIDTH]
    cm = xbc[:, D_INNER + BC_WIDTH:]

    dt = dt_full[:, :HEADS]
    da = dt * (-jnp.exp(alog_ref[...]))
    a_cum = _dot(tri_ref[...], jnp.concatenate(_split3(da), axis=0))
    both = jnp.concatenate([a_cum * LOG2_E, dt], axis=0)
    terms = jnp.concatenate([v.astype(F32) for v in _split3(both)], axis=1).astype(BF16)
    expanded = _dot(terms, e_ref[...])
    a_e = expanded[:lc]
    dt_e = expanded[lc:]
    diag = diag_ref[...]
    a_j = jnp.sum(a_e * diag, axis=0, keepdims=True)
    dt_j = jnp.sum(dt_e * diag, axis=0, keepdims=True)
    lmat = jnp.exp2(a_e - a_j + causal_ref[...])
    a_last = a_e[lc - 1:lc, :]
    exp_a = jnp.exp2(a_e)
    xw = (xs * (jnp.exp2(a_last - a_e) * dt_e)).astype(BF16)
    chunk_decay = jnp.exp2(a_last)

    blk_r = lax.broadcasted_iota(jnp.int32, (GROUP_COLS, GROUP_COLS), 0) // HEADDIM
    blk_c = lax.broadcasted_iota(jnp.int32, (GROUP_COLS, GROUP_COLS), 1) // HEADDIM
    same_head = blk_r == blk_c
    nt = (((1,), (1,)), ((), ()))
    tn = (((0,), (0,)), ((), ()))
    ys = []
    for g in range(SSD_GROUPS):
        cs = slice(g * GROUP_COLS, (g + 1) * GROUP_COLS)
        ns = slice(g * SSD_STATE, (g + 1) * SSD_STATE)
        bm_g, cm_g = bm[:, ns], cm[:, ns]
        cb = lax.dot_general(cm_g, jnp.concatenate([bm_g] * HPG, axis=0), nt,
                             preferred_element_type=F32)
        w = (cb * lmat[:, cs] * dt_j[:, cs]).astype(BF16)
        x_g = xs_b[:, cs]
        x_bd = jnp.where(same_head, jnp.concatenate([x_g] * HPG, axis=0), jnp.zeros((), BF16))
        y_diag = _dot(w, x_bd)
        h_prev = h_ref[g]
        y_off = _dot(cm_g, h_prev.astype(BF16)) * exp_a[:, cs]
        ys.append(y_diag + y_off)
        st = lax.dot_general(bm_g, xw[:, cs], tn, preferred_element_type=F32)
        h_ref[g] = h_prev * chunk_decay[:, cs] + st

    return (jnp.concatenate(ys, axis=1) + dexp_ref[...] * xs).astype(BF16)


def _ssd_body(xbc_ref, dt_ref, alog_ref, dexp_ref, e_ref, tri_ref, causal_ref, diag_ref, y_ref, h_ref, *, cps):
    lc = SSD_CHUNK

    @pl.when(pl.program_id(1) == 0)
    def _():
        h_ref[...] = jnp.zeros(h_ref.shape, F32)

    for ci in range(cps):
        rs = slice(ci * lc, (ci + 1) * lc)
        y_ref[rs, :] = _ssd_chunk(xbc_ref[rs, :], dt_ref[rs, :], alog_ref, dexp_ref, e_ref, tri_ref,
                                  causal_ref, diag_ref, h_ref)


def _ssd(xbc, dt, alog, dexp, e, tri, causal, diag, *, batch, cps=8):
    t = xbc.shape[0]
    seq = t // batch
    cps = min(cps, seq // SSD_CHUNK)
    rows = cps * SSD_CHUNK
    ns = seq // rows

    def tile(c):
        return pl.BlockSpec((rows, c), lambda b, i: (b * ns + i, 0))

    return pl.pallas_call(
        functools.partial(_ssd_body, cps=cps),
        grid=(batch, ns),
        in_specs=[tile(CONV_DIM), tile(DT_PAD)] + [_whole()] * 6,
        out_specs=tile(D_INNER),
        out_shape=jax.ShapeDtypeStruct((t, D_INNER), BF16),
        scratch_shapes=[pltpu.VMEM((SSD_GROUPS, SSD_STATE, GROUP_COLS), F32)],
        compiler_params=_params(("parallel", "arbitrary")),
        name="ssd",
    )(xbc, dt, alog, dexp, e, tri, causal, diag)


def _ssd_constants():
    r = np.arange(SSD_CHUNK)
    time = SUBLANES * (r % SUBLANES) + r // SUBLANES
    before = time[None, :] <= time[:, None]
    tri = np.tile(before.astype(np.float32), (1, 3))
    causal = np.where(np.tile(before, (1, HEADS)), 0.0, -np.inf).astype(np.float32)
    diag = np.tile(np.eye(SSD_CHUNK, dtype=np.float32), (1, HEADS))
    e_mat = np.tile(np.repeat(np.eye(HEADS, dtype=np.float32), HEADDIM, axis=1), (3, 1))
    return (jnp.asarray(e_mat, BF16), jnp.asarray(tri, BF16), jnp.asarray(causal), jnp.asarray(diag))


def _merge_body(x_ref, y5_ref, yssd_ref, gm_ref, wz_ref, wgt_ref, nw_ref, bg_ref, wglu_ref, bglu_ref,
                wp5_ref, wpssd_ref, wout_ref, o_ref):
    x = x_ref[...]
    h = (x * _rms_scale(x) * gm_ref[...]).astype(BF16)
    y = _block_transpose(y5_ref[...])
    gl = 0.5 * y * (1.0 + jnp.tanh(np.float32(np.sqrt(2.0 / np.pi)) * (y + 0.044715 * (y * y * y))))
    glu = gl * _sigmoid(_dot(gl.astype(BF16), wglu_ref[...].astype(BF16)) + bglu_ref[...])
    p5 = _dot(glu.astype(BF16), wp5_ref[...].astype(BF16))
    t = yssd_ref[...].astype(F32) * _silu(_dot(h, wz_ref[...]))
    tn = (t * _rms_scale(t) * nw_ref[...]).astype(BF16)
    pssd = _dot(tn, wpssd_ref[...].astype(BF16))
    g5 = _sigmoid(_dot(h, wgt_ref[:, :D_MODEL]) + bg_ref[:, :D_MODEL])
    gssd = _sigmoid(_dot(h, wgt_ref[:, D_MODEL:]) + bg_ref[:, D_MODEL:])
    merged = g5 * p5 + gssd * pssd
    o_ref[...] = x + _dot(merged.astype(BF16), wout_ref[...].astype(BF16))


def _merge(x, y5, yssd, gm, wz, wgt, nw, bg, wglu, bglu, wp5, wpssd, wout, *, tm=256):
    t = x.shape[0]
    tm = min(tm, t)

    def tile(c):
        return pl.BlockSpec((tm, c), lambda i: (i, 0))

    return pl.pallas_call(
        _merge_body,
        grid=(t // tm,),
        in_specs=[tile(D_MODEL), tile(D_MODEL), tile(D_INNER)] + [_whole()] * 10,
        out_specs=tile(D_MODEL),
        out_shape=jax.ShapeDtypeStruct((t, D_MODEL), F32),
        compiler_params=_params(("parallel",)),
        name="merge",
    )(x, y5, yssd, gm, wz, wgt, nw, bg, wglu, bglu, wp5, wpssd, wout)


def kernel(x, ffn1_norm, ffn1_w_gate, ffn1_w_up, ffn1_w_down, mix_norm, w_in, conv_w, conv_b, s5_A_re, s5_A_im, s5_log_dt, s5_B_re, s5_B_im, s5_C_re, s5_C_im, s5_D, s5_w_glu, s5_b_glu, ssd_A_log, ssd_dt_bias, ssd_D, ssd_norm, w_proj_s5, w_proj_ssd, b_gate, w_out, ffn2_norm, ffn2_w_gate, ffn2_w_up, ffn2_w_down, final_norm):
    bsz, seq, _ = x.shape
    assert SUBLANES % bsz == 0, "the S5 row interleave needs the batch to divide the sublane count"
    assert seq % ROW_BLOCK == 0
    t = bsz * seq
    depth = ffn1_norm.shape[0]
    xt = x.reshape(t, D_MODEL)
    row = lambda v: v.reshape(1, -1).astype(F32)
    fin = row(final_norm)
    e_mat, tri, causal, diag = _ssd_constants()
    dt_end = MAIN_COLS + HEADS

    for i in range(depth):
        xt = _ffn(xt, row(ffn1_norm[i]), ffn1_w_gate[i], ffn1_w_up[i], ffn1_w_down[i], fin,
                  first=(i == 0), final=False)

        w = w_in[i]
        w_dt = jnp.pad(w[:, MAIN_COLS:dt_end], ((0, 0), (0, DT_PAD - HEADS))).astype(BF16)
        dtb = jnp.pad(ssd_dt_bias[i], (0, DT_PAD - HEADS)).reshape(1, DT_PAD)
        z_end = D_MODEL + D_INNER
        u, xbc, dt = _inproj(xt, row(mix_norm[i]), w[:, :D_MODEL].astype(BF16), w[:, z_end:MAIN_COLS].astype(BF16),
                             w_dt, conv_w[i], row(conv_b[i]), dtb, batch=bsz)

        w1, cbd, apow = _s5_operators(s5_A_re[i], s5_A_im[i], s5_log_dt[i], s5_B_re[i], s5_B_im[i],
                                      s5_C_re[i], s5_C_im[i], SUBLANES // bsz)
        y5 = _s5(u.reshape(bsz, seq, D_MODEL), w1, cbd, apow, row(s5_D[i])).reshape(t, D_MODEL)

        yssd = _ssd(xbc, dt, row(ssd_A_log[i]), row(jnp.repeat(ssd_D[i], HEADDIM)), e_mat, tri, causal, diag,
                    batch=bsz)

        xt = _merge(xt, y5, yssd, row(mix_norm[i]), w[:, D_MODEL:z_end].astype(BF16), w[:, dt_end:].astype(BF16),
                    row(ssd_norm[i]), row(b_gate[i]), s5_w_glu[i], row(s5_b_glu[i]), w_proj_s5[i],
                    w_proj_ssd[i], w_out[i])

        xt = _ffn(xt, row(ffn2_norm[i]), ffn2_w_gate[i], ffn2_w_up[i], ffn2_w_down[i], fin,
                  first=False, final=(i == depth - 1))
    return xt.reshape(bsz, seq, D_MODEL)
```

```python
import functools

import jax
import jax.numpy as jnp
import numpy as np
from jax import lax
from jax.experimental import pallas as pl
from jax.experimental.pallas import tpu as pltpu

EPS = 1e-6
LOG2_E = float(np.log2(np.e))
D_MODEL = 1024
D_FF = 2816
S5_GROUP = 16
S5_GROUPS = 64
S5_STATE = 64
D_INNER = 2048
HEADDIM = 64
HEADS = 32
SSD_GROUPS = 8
HPG = 4
SSD_STATE = 128
CONV_K = 4
CONV_DIM = 4096
BC_WIDTH = SSD_GROUPS * SSD_STATE
GROUP_COLS = HPG * HEADDIM

LANES = 128
SUBLANES = 8
ROW_BLOCK = SUBLANES * SUBLANES
S5_BLOCK_GROUPS = LANES // S5_GROUP
S5_BLOCKS = S5_GROUPS // S5_BLOCK_GROUPS
S5_BLOCK_STATE = S5_BLOCK_GROUPS * S5_STATE
S5_PAIR = 2
S5_SUB_ROWS = 256
SSD_CHUNK = ROW_BLOCK
DT_PAD = LANES
PROJ_COLS = 512
FF_COLS = 256
MAIN_COLS = D_MODEL + D_INNER + CONV_DIM

VMEM_LIMIT = 56 * 1024 * 1024

F32 = jnp.float32
BF16 = jnp.bfloat16


def _dot(a, b):
    return jnp.dot(a, b, preferred_element_type=F32)


def _sigmoid(v):
    return 1.0 / (1.0 + jnp.exp(-v))


def _silu(v):
    return v * _sigmoid(v)


def _rms_scale(xf):
    return lax.rsqrt(jnp.mean(xf * xf, axis=-1, keepdims=True) + EPS)


def _split3(v):
    hi = v.astype(BF16)
    r1 = v - hi.astype(F32)
    mid = r1.astype(BF16)
    lo = (r1 - mid.astype(F32)).astype(BF16)
    return hi, mid, lo


def _block_transpose(v):
    rows, cols = v.shape
    return jnp.swapaxes(v.reshape(rows // ROW_BLOCK, SUBLANES, SUBLANES, cols), 1, 2).reshape(rows, cols)


def _whole():
    return pl.BlockSpec(memory_space=pltpu.VMEM)


def _params(sem):
    return pltpu.CompilerParams(dimension_semantics=sem, vmem_limit_bytes=VMEM_LIMIT)


def _ffn_body(x_ref, g_ref, wg_ref, wu_ref, wd_ref, fin_ref, o_ref, *, first, final):
    x = _block_transpose(x_ref[...]) if first else x_ref[...]
    h = (x * _rms_scale(x) * g_ref[...]).astype(BF16)
    acc = jnp.zeros(x.shape, F32)
    for c0 in range(0, D_FF, FF_COLS):
        sl = slice(c0, c0 + FF_COLS)
        a = _dot(h, wg_ref[:, sl].astype(BF16))
        b = _dot(h, wu_ref[:, sl].astype(BF16))
        act = (_silu(a) * b).astype(BF16)
        acc = acc + _dot(act, wd_ref[sl, :].astype(BF16))
    y = x + 0.5 * acc
    if final:
        y = y * _rms_scale(y) * fin_ref[...]
        o_ref[...] = _block_transpose(y)
    else:
        o_ref[...] = y


def _ffn(x, g, wg, wu, wd, fin, *, first, final, tm=512):
    t = x.shape[0]
    tm = min(tm, t)
    tile = pl.BlockSpec((tm, D_MODEL), lambda i: (i, 0))
    return pl.pallas_call(
        functools.partial(_ffn_body, first=first, final=final),
        grid=(t // tm,),
        in_specs=[tile, _whole(), _whole(), _whole(), _whole(), _whole()],
        out_specs=tile,
        out_shape=jax.ShapeDtypeStruct((t, D_MODEL), F32),
        compiler_params=_params(("parallel",)),
        name="ffn_final" if final else "ffn",
    )(x, g, wg, wu, wd, fin)


def _inproj_body(x_ref, g_ref, wu_ref, wx_ref, wdt_ref, cw_ref, cb_ref, dtb_ref,
                 u_ref, xbc_ref, dt_ref, tail_ref):
    tm = x_ref.shape[0]
    nblk = tm // ROW_BLOCK
    x = x_ref[...]
    h = (x * _rms_scale(x) * g_ref[...]).astype(BF16)
    for c0 in range(0, D_MODEL, PROJ_COLS):
        u_ref[:, c0:c0 + PROJ_COLS] = _block_transpose(_dot(h, wu_ref[:, c0:c0 + PROJ_COLS])).astype(BF16)
    raw = _dot(h, wdt_ref[...]) + dtb_ref[...]
    dt_ref[...] = jnp.maximum(raw, 0.0) + jnp.log1p(jnp.exp(-jnp.abs(raw)))

    @pl.when(pl.program_id(1) == 0)
    def _():
        tail_ref[...] = jnp.zeros(tail_ref.shape, F32)

    nd = CONV_K - 1
    sub = lax.broadcasted_iota(jnp.int32, (nblk, nd, SUBLANES, PROJ_COLS), 2)
    for c0 in range(0, CONV_DIM, PROJ_COLS):
        cs = slice(c0, c0 + PROJ_COLS)
        v = _dot(h, wx_ref[:, cs]).reshape(nblk, SUBLANES, SUBLANES, PROJ_COLS)
        rolled = pltpu.roll(v[:, SUBLANES - nd:].reshape(nblk * nd, SUBLANES, PROJ_COLS), 1, axis=1)
        rolled = rolled.reshape(nblk, nd, SUBLANES, PROJ_COLS)
        carry = tail_ref[:, cs].reshape(1, nd, SUBLANES, PROJ_COLS)
        tail_ref[:, cs] = rolled[nblk - 1].reshape(nd * SUBLANES, PROJ_COLS)
        prev = jnp.concatenate([carry, rolled[:nblk - 1]], axis=0) if nblk > 1 else carry
        edge = jnp.where(sub >= 1, rolled, prev)
        w_tap = lambda k: cw_ref[k:k + 1, cs].reshape(1, 1, 1, PROJ_COLS)
        acc = cb_ref[:, cs].reshape(1, 1, 1, PROJ_COLS) + w_tap(CONV_K - 1) * v
        for j in range(1, CONV_K):
            delayed = jnp.concatenate([edge[:, nd - j:], v[:, :SUBLANES - j]], axis=1)
            acc = acc + w_tap(CONV_K - 1 - j) * delayed
        xbc_ref[:, cs] = _silu(acc).reshape(tm, PROJ_COLS).astype(BF16)


def _inproj(x, g, wu, wx, wdt, cw, cb, dtb, *, batch, tm=512):
    t = x.shape[0]
    seq = t // batch
    tm = min(tm, seq)
    nt = seq // tm

    def tile(c):
        return pl.BlockSpec((tm, c), lambda b, i: (b * nt + i, 0))

    outs = ((D_MODEL, BF16), (CONV_DIM, BF16), (DT_PAD, F32))
    return pl.pallas_call(
        _inproj_body,
        grid=(batch, nt),
        in_specs=[tile(D_MODEL)] + [_whole()] * 7,
        out_specs=[tile(c) for c, _ in outs],
        out_shape=[jax.ShapeDtypeStruct((t, c), d) for c, d in outs],
        scratch_shapes=[pltpu.VMEM(((CONV_K - 1) * SUBLANES, CONV_DIM), F32)],
        compiler_params=_params(("parallel", "arbitrary")),
        name="inproj",
    )(x, g, wu, wx, wdt, cw, cb, dtb)


def _s5_body(u_ref, w1_ref, cbd_ref, apow_ref, dskip_ref, y_ref, ubuf_ref, bp_ref, xs_ref, x_ref, ybuf_ref,
             *, n_sub):
    bsz, tt, _ = u_ref.shape
    m = bsz * tt
    ns = S5_PAIR * S5_BLOCK_STATE
    i = pl.program_id(1)

    @pl.when(i == 0)
    def _():
        ubuf_ref[:, 0:SUBLANES, :] = jnp.zeros((S5_PAIR, SUBLANES, LANES), F32)
        x_ref[...] = jnp.zeros(x_ref.shape, F32)

    @pl.when(i > 0)
    def _():
        ubuf_ref[:, 0:SUBLANES, :] = ubuf_ref[:, m:m + SUBLANES, :]

    for q in range(S5_PAIR):
        for b in range(bsz):
            ubuf_ref[q, pl.ds(SUBLANES + b, tt, stride=bsz), :] = u_ref[b, :, q * LANES:(q + 1) * LANES].astype(F32)

    sub_rows = min(S5_SUB_ROWS, m)
    for r0 in range(0, m, sub_rows):
        for q in range(S5_PAIR):
            ucat = jnp.concatenate(
                [ubuf_ref[q, SUBLANES - k * bsz + r0:SUBLANES - k * bsz + r0 + sub_rows, :].astype(BF16)
                 for k in range(n_sub)], axis=1)
            bp_ref[r0:r0 + sub_rows, q * S5_BLOCK_STATE:(q + 1) * S5_BLOCK_STATE] = _dot(ucat, w1_ref[q, 0])
            bp_ref[r0:r0 + sub_rows, ns + q * S5_BLOCK_STATE:ns + (q + 1) * S5_BLOCK_STATE] = _dot(ucat, w1_ref[q, 1])

    a_r = jnp.broadcast_to(apow_ref[0, 0:1, :], (SUBLANES, ns))
    a_i = jnp.broadcast_to(apow_ref[0, 1:2, :], (SUBLANES, ns))
    pair = 2 * SUBLANES
    xr, xi = x_ref[:, 0:ns], x_ref[:, ns:2 * ns]
    for r in range(0, m, pair):
        re0 = a_r * xr - a_i * xi + bp_ref[r:r + SUBLANES, 0:ns]
        im0 = a_r * xi + a_i * xr + bp_ref[r:r + SUBLANES, ns:2 * ns]
        xr = a_r * re0 - a_i * im0 + bp_ref[r + SUBLANES:r + pair, 0:ns]
        xi = a_r * im0 + a_i * re0 + bp_ref[r + SUBLANES:r + pair, ns:2 * ns]
        xs_ref[r:r + pair, 0:ns] = jnp.concatenate([re0, xr], axis=0).astype(BF16)
        xs_ref[r:r + pair, ns:2 * ns] = jnp.concatenate([im0, xi], axis=0).astype(BF16)
    x_ref[:, 0:ns] = xr
    x_ref[:, ns:2 * ns] = xi

    for r0 in range(0, m, sub_rows):
        y = _dot(xs_ref[r0:r0 + sub_rows, :], cbd_ref[0])
        for q in range(S5_PAIR):
            cols = slice(q * LANES, (q + 1) * LANES)
            ybuf_ref[q, r0:r0 + sub_rows, :] = (
                y[:, cols] + dskip_ref[:, cols] * ubuf_ref[q, SUBLANES + r0:SUBLANES + r0 + sub_rows, :])
    for q in range(S5_PAIR):
        for b in range(bsz):
            y_ref[b, :, q * LANES:(q + 1) * LANES] = ybuf_ref[q, pl.ds(b, tt, stride=bsz), :]


def _s5(u3, w1, cbd, apow, dskip, *, tt=512):
    bsz, seq, _ = u3.shape
    n_sub = SUBLANES // bsz
    tt = min(tt, seq)
    m = bsz * tt
    width = S5_PAIR * LANES
    ns = S5_PAIR * S5_BLOCK_STATE
    blk = pl.BlockSpec((bsz, tt, width), lambda j, i: (0, i, j))
    return pl.pallas_call(
        functools.partial(_s5_body, n_sub=n_sub),
        grid=(S5_BLOCKS // S5_PAIR, seq // tt),
        in_specs=[blk,
                  pl.BlockSpec((S5_PAIR, 2, n_sub * LANES, S5_BLOCK_STATE), lambda j, i: (j, 0, 0, 0)),
                  pl.BlockSpec((1, 2 * ns, width), lambda j, i: (j, 0, 0)),
                  pl.BlockSpec((1, 2, ns), lambda j, i: (j, 0, 0)),
                  pl.BlockSpec((1, width), lambda j, i: (0, j))],
        out_specs=blk,
        out_shape=jax.ShapeDtypeStruct((bsz, seq, D_MODEL), F32),
        scratch_shapes=[pltpu.VMEM((S5_PAIR, m + SUBLANES, LANES), F32),
                        pltpu.VMEM((m, 2 * ns), F32),
                        pltpu.VMEM((m, 2 * ns), BF16),
                        pltpu.VMEM((SUBLANES, 2 * ns), F32),
                        pltpu.VMEM((S5_PAIR, m, LANES), F32)],
        compiler_params=_params(("parallel", "arbitrary")),
        name="s5",
    )(u3, w1, cbd, apow, dskip)


def _s5_operators(a_re, a_im, log_dt, b_re, b_im, c_re, c_im, n_sub):
    dt = jnp.exp(log_dt)[:, None]
    lr, li = a_re, a_im
    mag = jnp.exp(lr * dt)
    ar, ai = mag * jnp.cos(li * dt), mag * jnp.sin(li * dt)
    den = lr * lr + li * li
    cr = ((ar - 1.0) * lr + ai * li) / den
    ci = (ai * lr - (ar - 1.0) * li) / den
    bbr = cr[..., None] * b_re - ci[..., None] * b_im
    bbi = cr[..., None] * b_im + ci[..., None] * b_re

    def power(k):
        m = jnp.exp(k * lr * dt)
        return m * jnp.cos(k * li * dt), m * jnp.sin(k * li * dt)

    eye = jnp.eye(S5_BLOCK_GROUPS, dtype=F32)

    def blockdiag_in(w):
        w = w.reshape(S5_BLOCKS, S5_BLOCK_GROUPS, S5_STATE, S5_GROUP)
        return jnp.einsum('jgnm,gh->jgmhn', w, eye).reshape(S5_BLOCKS, LANES, S5_BLOCK_STATE)

    def blockdiag_out(c):
        c = c.reshape(S5_BLOCKS, S5_BLOCK_GROUPS, S5_GROUP, S5_STATE)
        return jnp.einsum('jgmn,gh->jgnhm', c, eye).reshape(S5_BLOCKS, S5_BLOCK_STATE, LANES)

    rows = []
    for k in range(n_sub):
        pr, pi = power(float(k))
        wr = pr[..., None] * bbr - pi[..., None] * bbi
        wi = pr[..., None] * bbi + pi[..., None] * bbr
        rows.append(jnp.stack([blockdiag_in(wr), blockdiag_in(wi)], axis=1))
    w1 = jnp.concatenate(rows, axis=2).astype(BF16)
    eye_p = jnp.eye(S5_PAIR, dtype=F32)

    def pair_out(c):
        c = c.reshape(S5_BLOCKS // S5_PAIR, S5_PAIR, S5_BLOCK_STATE, LANES)
        return jnp.einsum('jqnm,qr->jqnrm', c, eye_p).reshape(S5_BLOCKS // S5_PAIR, S5_PAIR * S5_BLOCK_STATE,
                                                              S5_PAIR * LANES)

    cbd = jnp.concatenate([pair_out(blockdiag_out(c_re)), pair_out(blockdiag_out(-c_im))], axis=1).astype(BF16)
    sr, si = power(float(n_sub))
    apow = jnp.stack([sr.reshape(S5_BLOCKS // S5_PAIR, S5_PAIR * S5_BLOCK_STATE),
                      si.reshape(S5_BLOCKS // S5_PAIR, S5_PAIR * S5_BLOCK_STATE)], axis=1)
    return w1, cbd, apow


def _ssd_chunk(xbc, dt_full, alog_ref, dexp_ref, e_ref, tri_ref, causal_ref, diag_ref, h_ref):
    lc = SSD_CHUNK
    xs_b = xbc[:, :D_INNER]
    xs = xs_b.astype(F32)
    bm = xbc[:, D_INNER:D_INNER + BC_WIDTH]
    cm = xbc[:, D_INNER + BC_WIDTH:]

    dt = dt_full[:, :HEADS]
    da = dt * (-jnp.exp(alog_ref[...]))
    a_cum = _dot(tri_ref[...], jnp.concatenate(_split3(da), axis=0))
    both = jnp.concatenate([a_cum * LOG2_E, dt], axis=0)
    terms = jnp.concatenate([v.astype(F32) for v in _split3(both)], axis=1).astype(BF16)
    expanded = _dot(terms, e_ref[...])
    a_e = expanded[:lc]
    dt_e = expanded[lc:]
    diag = diag_ref[...]
    a_j = jnp.sum(a_e * diag, axis=0, keepdims=True)
    dt_j = jnp.sum(dt_e * diag, axis=0, keepdims=True)
    lmat = jnp.exp2(a_e - a_j + causal_ref[...])
    a_last = a_e[lc - 1:lc, :]
    exp_a = jnp.exp2(a_e)
    xw = (xs * (jnp.exp2(a_last - a_e) * dt_e)).astype(BF16)
    chunk_decay = jnp.exp2(a_last)

    blk_r = lax.broadcasted_iota(jnp.int32, (GROUP_COLS, GROUP_COLS), 0) // HEADDIM
    blk_c = lax.broadcasted_iota(jnp.int32, (GROUP_COLS, GROUP_COLS), 1) // HEADDIM
    same_head = blk_r == blk_c
    nt = (((1,), (1,)), ((), ()))
    tn = (((0,), (0,)), ((), ()))
    ys = []
    for g in range(SSD_GROUPS):
        cs = slice(g * GROUP_COLS, (g + 1) * GROUP_COLS)
        ns = slice(g * SSD_STATE, (g + 1) * SSD_STATE)
        bm_g, cm_g = bm[:, ns], cm[:, ns]
        cb = lax.dot_general(cm_g, jnp.concatenate([bm_g] * HPG, axis=0), nt,
                             preferred_element_type=F32)
        w = (cb * lmat[:, cs] * dt_j[:, cs]).astype(BF16)
        x_g = xs_b[:, cs]
        x_bd = jnp.where(same_head, jnp.concatenate([x_g] * HPG, axis=0), jnp.zeros((), BF16))
        y_diag = _dot(w, x_bd)
        h_prev = h_ref[g]
        y_off = _dot(cm_g, h_prev.astype(BF16)) * exp_a[:, cs]
        ys.append(y_diag + y_off)
        st = lax.dot_general(bm_g, xw[:, cs], tn, preferred_element_type=F32)
        h_ref[g] = h_prev * chunk_decay[:, cs] + st

    return (jnp.concatenate(ys, axis=1) + dexp_ref[...] * xs).astype(BF16)


def _ssd_body(xbc_ref, dt_ref, alog_ref, dexp_ref, e_ref, tri_ref, causal_ref, diag_ref, y_ref, h_ref, *, cps):
    lc = SSD_CHUNK

    @pl.when(pl.program_id(1) == 0)
    def _():
        h_ref[...] = jnp.zeros(h_ref.shape, F32)

    for ci in range(cps):
        rs = slice(ci * lc, (ci + 1) * lc)
        y_ref[rs, :] = _ssd_chunk(xbc_ref[rs, :], dt_ref[rs, :], alog_ref, dexp_ref, e_ref, tri_ref,
                                  causal_ref, diag_ref, h_ref)


def _ssd(xbc, dt, alog, dexp, e, tri, causal, diag, *, batch, cps=8):
    t = xbc.shape[0]
    seq = t // batch
    cps = min(cps, seq // SSD_CHUNK)
    rows = cps * SSD_CHUNK
    ns = seq // rows

    def tile(c):
        return pl.BlockSpec((rows, c), lambda b, i: (b * ns + i, 0))

    return pl.pallas_call(
        functools.partial(_ssd_body, cps=cps),
        grid=(batch, ns),
        in_specs=[tile(CONV_DIM), tile(DT_PAD)] + [_whole()] * 6,
        out_specs=tile(D_INNER),
        out_shape=jax.ShapeDtypeStruct((t, D_INNER), BF16),
        scratch_shapes=[pltpu.VMEM((SSD_GROUPS, SSD_STATE, GROUP_COLS), F32)],
        compiler_params=_params(("parallel", "arbitrary")),
        name="ssd",
    )(xbc, dt, alog, dexp, e, tri, causal, diag)


def _ssd_constants():
    r = np.arange(SSD_CHUNK)
    time = SUBLANES * (r % SUBLANES) + r // SUBLANES
    before = time[None, :] <= time[:, None]
    tri = np.tile(before.astype(np.float32), (1, 3))
    causal = np.where(np.tile(before, (1, HEADS)), 0.0, -np.inf).astype(np.float32)
    diag = np.tile(np.eye(SSD_CHUNK, dtype=np.float32), (1, HEADS))
    e_mat = np.tile(np.repeat(np.eye(HEADS, dtype=np.float32), HEADDIM, axis=1), (3, 1))
    return (jnp.asarray(e_mat, BF16), jnp.asarray(tri, BF16), jnp.asarray(causal), jnp.asarray(diag))


def _merge_body(x_ref, y5_ref, yssd_ref, gm_ref, wz_ref, wgt_ref, nw_ref, bg_ref, wglu_ref, bglu_ref,
                wp5_ref, wpssd_ref, wout_ref, o_ref):
    x = x_ref[...]
    h = (x * _rms_scale(x) * gm_ref[...]).astype(BF16)
    y = _block_transpose(y5_ref[...])
    gl = 0.5 * y * (1.0 + jnp.tanh(np.float32(np.sqrt(2.0 / np.pi)) * (y + 0.044715 * (y * y * y))))
    glu = gl * _sigmoid(_dot(gl.astype(BF16), wglu_ref[...].astype(BF16)) + bglu_ref[...])
    p5 = _dot(glu.astype(BF16), wp5_ref[...].astype(BF16))
    t = yssd_ref[...].astype(F32) * _silu(_dot(h, wz_ref[...]))
    tn = (t * _rms_scale(t) * nw_ref[...]).astype(BF16)
    pssd = _dot(tn, wpssd_ref[...].astype(BF16))
    g5 = _sigmoid(_dot(h, wgt_ref[:, :D_MODEL]) + bg_ref[:, :D_MODEL])
    gssd = _sigmoid(_dot(h, wgt_ref[:, D_MODEL:]) + bg_ref[:, D_MODEL:])
    merged = g5 * p5 + gssd * pssd
    o_ref[...] = x + _dot(merged.astype(BF16), wout_ref[...].astype(BF16))


def _merge(x, y5, yssd, gm, wz, wgt, nw, bg, wglu, bglu, wp5, wpssd, wout, *, tm=256):
    t = x.shape[0]
    tm = min(tm, t)

    def tile(c):
        return pl.BlockSpec((tm, c), lambda i: (i, 0))

    return pl.pallas_call(
        _merge_body,
        grid=(t // tm,),
        in_specs=[tile(D_MODEL), tile(D_MODEL), tile(D_INNER)] + [_whole()] * 10,
        out_specs=tile(D_MODEL),
        out_shape=jax.ShapeDtypeStruct((t, D_MODEL), F32),
        compiler_params=_params(("parallel",)),
        name="merge",
    )(x, y5, yssd, gm, wz, wgt, nw, bg, wglu, bglu, wp5, wpssd, wout)


def kernel(x, ffn1_norm, ffn1_w_gate, ffn1_w_up, ffn1_w_down, mix_norm, w_in, conv_w, conv_b, s5_A_re, s5_A_im, s5_log_dt, s5_B_re, s5_B_im, s5_C_re, s5_C_im, s5_D, s5_w_glu, s5_b_glu, ssd_A_log, ssd_dt_bias, ssd_D, ssd_norm, w_proj_s5, w_proj_ssd, b_gate, w_out, ffn2_norm, ffn2_w_gate, ffn2_w_up, ffn2_w_down, final_norm):
    bsz, seq, _ = x.shape
    assert SUBLANES % bsz == 0, "the S5 row interleave needs the batch to divide the sublane count"
    assert seq % ROW_BLOCK == 0
    t = bsz * seq
    depth = ffn1_norm.shape[0]
    xt = x.reshape(t, D_MODEL)
    row = lambda v: v.reshape(1, -1).astype(F32)
    fin = row(final_norm)
    e_mat, tri, causal, diag = _ssd_constants()
    dt_end = MAIN_COLS + HEADS

    for i in range(depth):
        xt = _ffn(xt, row(ffn1_norm[i]), ffn1_w_gate[i], ffn1_w_up[i], ffn1_w_down[i], fin,
                  first=(i == 0), final=False)

        w = w_in[i]
        w_dt = jnp.pad(w[:, MAIN_COLS:dt_end], ((0, 0), (0, DT_PAD - HEADS))).astype(BF16)
        dtb = jnp.pad(ssd_dt_bias[i], (0, DT_PAD - HEADS)).reshape(1, DT_PAD)
        z_end = D_MODEL + D_INNER
        u, xbc, dt = _inproj(xt, row(mix_norm[i]), w[:, :D_MODEL].astype(BF16), w[:, z_end:MAIN_COLS].astype(BF16),
                             w_dt, conv_w[i], row(conv_b[i]), dtb, batch=bsz)

        w1, cbd, apow = _s5_operators(s5_A_re[i], s5_A_im[i], s5_log_dt[i], s5_B_re[i], s5_B_im[i],
                                      s5_C_re[i], s5_C_im[i], SUBLANES // bsz)
        y5 = _s5(u.reshape(bsz, seq, D_MODEL), w1, cbd, apow, row(s5_D[i])).reshape(t, D_MODEL)

        yssd = _ssd(xbc, dt, row(ssd_A_log[i]), row(jnp.repeat(ssd_D[i], HEADDIM)), e_mat, tri, causal, diag,
                    batch=bsz)

        xt = _merge(xt, y5, yssd, row(mix_norm[i]), w[:, D_MODEL:z_end].astype(BF16), w[:, dt_end:].astype(BF16),
                    row(ssd_norm[i]), row(b_gate[i]), s5_w_glu[i], row(s5_b_glu[i]), w_proj_s5[i],
                    w_proj_ssd[i], w_out[i])

        xt = _ffn(xt, row(ffn2_norm[i]), ffn2_w_gate[i], ffn2_w_up[i], ffn2_w_down[i], fin,
                  first=False, final=(i == depth - 1))
    return xt.reshape(bsz, seq, D_MODEL)
```
